```python
import numpy as np
import jax, jax.numpy as jnp
from jax import lax

D_MODEL = 1024
BATCH = 8
SEQ = 8192
DEPTH = 1
DEC_BATCH = 16
DEC_SEQ = 64
PAST_LEN = 4096

CHUNK = 64
GDN_HEADS = 8
GDN_DK = 128
GDN_DV = 128
GDN_CONV = 4
GDN_QKV = 2 * GDN_HEADS * GDN_DK + GDN_HEADS * GDN_DV
SC_WIDTH = D_MODEL
SC_CONV = 3
N_MEM = 256
XA_HEADS = 4
XA_DIM = D_MODEL // XA_HEADS
D_FF = 2816
EPS = 1e-6
MIX_SPLIT = (GDN_HEADS * GDN_DK, GDN_HEADS * GDN_DK, GDN_HEADS * GDN_DV, GDN_HEADS * GDN_DV,
             GDN_HEADS, GDN_HEADS, SC_WIDTH, SC_WIDTH, SC_WIDTH, D_MODEL, D_MODEL)
MIX_IN = 4 * GDN_HEADS * GDN_DK + 2 * GDN_HEADS + 3 * SC_WIDTH + 2 * D_MODEL

kernel_name = 'hybrid_gdn_shortconv_streaming_encoder'


def rmsnorm(x, g):
    xf = x.astype(jnp.float32)
    y = xf * lax.rsqrt(jnp.mean(xf * xf, axis=-1, keepdims=True) + EPS)
    return (y * g.astype(jnp.float32)).astype(x.dtype)


def l2norm(x):
    return x * lax.rsqrt(jnp.sum(x * x, axis=-1, keepdims=True) + EPS)


def swiglu(x, w_in, w_out):
    gate, up = jnp.split(x @ w_in, 2, axis=-1)
    return (jax.nn.silu(gate) * up) @ w_out


def causal_dwconv(x, buf, w):
    width = w.shape[0]
    t = x.shape[1]
    xp = jnp.concatenate([buf.astype(x.dtype), x], axis=1)
    y = sum(w[i] * xp[:, i:i + t] for i in range(width))
    return y, xp[:, t:]


def gdn_chunk(S, xs):
    q, k, v, g, beta = xs
    L = q.shape[1]
    q, k, v = (jnp.moveaxis(a, 1, 2) for a in (q, k, v))
    gc = jnp.cumsum(jnp.moveaxis(g, 1, 2), axis=-1)
    beta = jnp.moveaxis(beta, 1, 2)
    pos = jnp.arange(L)
    causal = pos[:, None] >= pos[None, :]
    decay = jnp.exp(jnp.where(causal, gc[..., :, None] - gc[..., None, :], -jnp.inf))
    kb = k * beta[..., None]
    a_mat = jnp.where(pos[:, None] > pos[None, :], jnp.einsum('bhlk,bhmk->bhlm', kb, k) * decay, 0.0)
    rhs = jnp.concatenate([v * beta[..., None], kb * jnp.exp(gc)[..., None]], axis=-1)
    sol = lax.linalg.triangular_solve(jnp.eye(L, dtype=a_mat.dtype) + a_mat, rhs,
                                      left_side=True, lower=True)
    u = sol[..., :GDN_DV] - jnp.einsum('bhlk,bhkv->bhlv', sol[..., GDN_DV:], S)
    qk = jnp.einsum('bhlk,bhmk->bhlm', q, k) * decay
    o = (jnp.einsum('bhlk,bhkv->bhlv', q * jnp.exp(gc)[..., None], S)
         + jnp.einsum('bhlm,bhmv->bhlv', qk, u))
    g_end = gc[..., -1:]
    S_new = (S * jnp.exp(g_end)[..., None]
             + jnp.einsum('bhlk,bhlv->bhkv', k * jnp.exp(g_end - gc)[..., None], u))
    return S_new, jnp.moveaxis(o, 2, 1)


def gdn_recurrence(S, q, k, v, g, beta):
    b, t = q.shape[0], q.shape[1]
    L = min(CHUNK, t)
    nc = t // L

    def to_chunks(a):
        return jnp.moveaxis(a.reshape((b, nc, L) + a.shape[2:]), 1, 0)

    S_new, o = lax.scan(gdn_chunk, S, tuple(to_chunks(a) for a in (q, k, v, g, beta)))
    return jnp.moveaxis(o, 0, 1).reshape(b, t, GDN_HEADS, GDN_DV), S_new


def token_mixer(h, S, gbuf, sbuf, p):
    b, t, _ = h.shape
    cuts = [int(c) for c in np.cumsum(MIX_SPLIT)[:-1]]
    q, k, v, z, a, bb, sB, sC, sx, gA, gB = jnp.split(h @ p['w_mix_in'], cuts, axis=-1)
    qkv, gbuf_new = causal_dwconv(jnp.concatenate([q, k, v], axis=-1), gbuf, p['w_gdn_conv'])
    qkv = jax.nn.silu(qkv).astype(jnp.float32)
    q, k, v = jnp.split(qkv, [GDN_HEADS * GDN_DK, 2 * GDN_HEADS * GDN_DK], axis=-1)
    q = l2norm(q.reshape(b, t, GDN_HEADS, GDN_DK)) * (GDN_DK ** -0.5)
    k = l2norm(k.reshape(b, t, GDN_HEADS, GDN_DK))
    v = v.reshape(b, t, GDN_HEADS, GDN_DV)
    g = -jnp.exp(p['gdn_a_log'].astype(jnp.float32)) * jax.nn.softplus(
        a.astype(jnp.float32) + p['gdn_dt_bias'].astype(jnp.float32))
    beta = jax.nn.sigmoid(bb.astype(jnp.float32))
    o, S_new = gdn_recurrence(S.astype(jnp.float32), q, k, v, g, beta)
    o = rmsnorm(o, p['gdn_norm']) * jax.nn.silu(z.reshape(b, t, GDN_HEADS, GDN_DV).astype(jnp.float32))
    y_a = o.reshape(b, t, GDN_HEADS * GDN_DV).astype(h.dtype) @ p['w_gdn_out']
    u, sbuf_new = causal_dwconv(sC * sx, sbuf, p['w_sc_conv'])
    y_b = (sB * u) @ p['w_sc_out']
    y = jax.nn.sigmoid(gA) * y_a + jax.nn.sigmoid(gB) * y_b
    return y @ p['w_mix_out'], S_new, gbuf_new, sbuf_new


def memory_kv(mem, p):
    b, m, _ = mem.shape
    k, v = jnp.split(rmsnorm(mem, p['n_mem']) @ p['w_xkv'], 2, axis=-1)
    return k.reshape(b, m, XA_HEADS, XA_DIM), v.reshape(b, m, XA_HEADS, XA_DIM)


def cross_attn(h, mk, mv, p):
    b, t, _ = h.shape
    q = (h @ p['w_xq']).reshape(b, t, XA_HEADS, XA_DIM)
    s = jnp.einsum('bthd,bmhd->bhtm', q, mk.astype(h.dtype)).astype(jnp.float32) * (XA_DIM ** -0.5)
    pr = jax.nn.softmax(s, axis=-1).astype(h.dtype)
    o = jnp.einsum('bhtm,bmhd->bthd', pr, mv.astype(h.dtype)).reshape(b, t, D_MODEL)
    return o @ p['w_xo']


def layer(x, mk, mv, S, gbuf, sbuf, p):
    x = x + 0.5 * rmsnorm(swiglu(rmsnorm(x, p['n_ffn1_pre']), p['w_ffn1_in'], p['w_ffn1_out']), p['n_ffn1_post'])
    m, S, gbuf, sbuf = token_mixer(rmsnorm(x, p['n_mix_pre']), S, gbuf, sbuf, p)
    x = x + rmsnorm(m, p['n_mix_post'])
    x = x + rmsnorm(cross_attn(rmsnorm(x, p['n_x_pre']), mk, mv, p), p['n_x_post'])
    x = x + 0.5 * rmsnorm(swiglu(rmsnorm(x, p['n_ffn2_pre']), p['w_ffn2_in'], p['w_ffn2_out']), p['n_ffn2_post'])
    return x, S, gbuf, sbuf


def setup_inputs(seed: int = 0) -> dict:
    key = jax.random.key(seed)
    ks = jax.random.split(key, 34)
    f32 = jnp.float32

    def nrm(i, shape, scale):
        return jax.random.normal(ks[i], shape, f32) * scale

    def wt(i, shape, fan_in):
        return nrm(i, (DEPTH,) + shape, fan_in ** -0.5)

    def gain(i, n):
        return 1.0 + nrm(i, (DEPTH, n), 0.02)

    a_log = jnp.log(jax.random.uniform(ks[15], (DEPTH, GDN_HEADS), f32, 1.0, 16.0))
    dt = jnp.exp(jax.random.uniform(ks[16], (DEPTH, GDN_HEADS), f32, float(np.log(1e-3)), float(np.log(1e-1))))
    dt_bias = dt + jnp.log(-jnp.expm1(-dt))
    return {
        'x_prompt': nrm(0, (BATCH, SEQ, D_MODEL), 1.0),
        'x_sample': nrm(1, (DEC_BATCH, DEC_SEQ, D_MODEL), 1.0),
        'mem_prompt': nrm(2, (BATCH, N_MEM, D_MODEL), 1.0),
        'cache_mem_k': nrm(3, (DEPTH, DEC_BATCH, N_MEM, XA_HEADS, XA_DIM), 1.0),
        'cache_mem_v': nrm(4, (DEPTH, DEC_BATCH, N_MEM, XA_HEADS, XA_DIM), 1.0),
        'state_gdn': nrm(5, (DEPTH, DEC_BATCH, GDN_HEADS, GDN_DK, GDN_DV), 0.1),
        'state_gdn_conv': nrm(6, (DEPTH, DEC_BATCH, GDN_CONV - 1, GDN_QKV), 1.0),
        'state_shortconv': nrm(7, (DEPTH, DEC_BATCH, SC_CONV - 1, SC_WIDTH), 0.5),
        'n_ffn1_pre': gain(8, D_MODEL),
        'w_ffn1_in': wt(9, (D_MODEL, 2 * D_FF), D_MODEL),
        'w_ffn1_out': wt(10, (D_FF, D_MODEL), D_FF),
        'n_ffn1_post': gain(11, D_MODEL),
        'n_mix_pre': gain(12, D_MODEL),
        'w_mix_in': wt(13, (D_MODEL, MIX_IN), D_MODEL),
        'w_gdn_conv': wt(14, (GDN_CONV, GDN_QKV), GDN_CONV),
        'gdn_a_log': a_log,
        'gdn_dt_bias': dt_bias,
        'gdn_norm': gain(17, GDN_DV),
        'w_gdn_out': wt(18, (GDN_HEADS * GDN_DV, D_MODEL), GDN_HEADS * GDN_DV),
        'w_sc_conv': wt(19, (SC_CONV, SC_WIDTH), SC_CONV),
        'w_sc_out': wt(20, (SC_WIDTH, D_MODEL), SC_WIDTH),
        'w_mix_out': wt(21, (D_MODEL, D_MODEL), D_MODEL),
        'n_mix_post': gain(22, D_MODEL),
        'n_x_pre': gain(23, D_MODEL),
        'n_mem': gain(24, D_MODEL),
        'w_xq': wt(25, (D_MODEL, D_MODEL), D_MODEL),
        'w_xkv': wt(26, (D_MODEL, 2 * D_MODEL), D_MODEL),
        'w_xo': wt(27, (D_MODEL, D_MODEL), D_MODEL),
        'n_x_post': gain(28, D_MODEL),
        'n_ffn2_pre': gain(29, D_MODEL),
        'w_ffn2_in': wt(30, (D_MODEL, 2 * D_FF), D_MODEL),
        'w_ffn2_out': wt(31, (D_FF, D_MODEL), D_FF),
        'n_ffn2_post': gain(32, D_MODEL),
    }


def reference(x_prompt, x_sample, mem_prompt, cache_mem_k, cache_mem_v, state_gdn, state_gdn_conv,
              state_shortconv, n_ffn1_pre, w_ffn1_in, w_ffn1_out, n_ffn1_post, n_mix_pre, w_mix_in,
              w_gdn_conv, gdn_a_log, gdn_dt_bias, gdn_norm, w_gdn_out, w_sc_conv, w_sc_out, w_mix_out,
              n_mix_post, n_x_pre, n_mem, w_xq, w_xkv, w_xo, n_x_post, n_ffn2_pre, w_ffn2_in, w_ffn2_out,
              n_ffn2_post):
    params = dict(n_ffn1_pre=n_ffn1_pre, w_ffn1_in=w_ffn1_in, w_ffn1_out=w_ffn1_out, n_ffn1_post=n_ffn1_post,
                  n_mix_pre=n_mix_pre, w_mix_in=w_mix_in, w_gdn_conv=w_gdn_conv, gdn_a_log=gdn_a_log,
                  gdn_dt_bias=gdn_dt_bias, gdn_norm=gdn_norm, w_gdn_out=w_gdn_out, w_sc_conv=w_sc_conv,
                  w_sc_out=w_sc_out, w_mix_out=w_mix_out, n_mix_post=n_mix_post, n_x_pre=n_x_pre,
                  n_mem=n_mem, w_xq=w_xq, w_xkv=w_xkv, w_xo=w_xo, n_x_post=n_x_post,
                  n_ffn2_pre=n_ffn2_pre, w_ffn2_in=w_ffn2_in, w_ffn2_out=w_ffn2_out, n_ffn2_post=n_ffn2_post)
    bp = x_prompt.shape[0]
    yp, ys = x_prompt, x_sample
    gdn_p, gconv_p, sconv_p, mk_p_all, mv_p_all = [], [], [], [], []
    gdn_s, gconv_s, sconv_s = [], [], []
    for l in range(DEPTH):
        p = {name: arr[l] for name, arr in params.items()}
        mk_p, mv_p = memory_kv(mem_prompt, p)
        s0 = jnp.zeros((bp, GDN_HEADS, GDN_DK, GDN_DV), jnp.float32)
        gb0 = jnp.zeros((bp, GDN_CONV - 1, GDN_QKV), x_prompt.dtype)
        sb0 = jnp.zeros((bp, SC_CONV - 1, SC_WIDTH), x_prompt.dtype)
        yp, sp, gbp, sbp = layer(yp, mk_p, mv_p, s0, gb0, sb0, p)
        ys, ss, gbs, sbs = layer(ys, cache_mem_k[l], cache_mem_v[l], state_gdn[l], state_gdn_conv[l],
                                 state_shortconv[l], p)
        gdn_p.append(sp); gconv_p.append(gbp); sconv_p.append(sbp)
        mk_p_all.append(mk_p); mv_p_all.append(mv_p)
        gdn_s.append(ss); gconv_s.append(gbs); sconv_s.append(sbs)
    return (yp, ys, jnp.stack(gdn_p), jnp.stack(gconv_p), jnp.stack(sconv_p), jnp.stack(mk_p_all),
            jnp.stack(mv_p_all), jnp.stack(gdn_s), jnp.stack(gconv_s), jnp.stack(sconv_s))
```

```python
import functools

import jax
import jax.numpy as jnp
from jax import lax
from jax.experimental import pallas as pl
from jax.experimental.pallas import tpu as pltpu

F32 = jnp.float32
BF16 = jnp.bfloat16

D_MODEL = 1024
D_FF = 2816
GDN_HEADS = 8
GDN_DK = 128
GDN_CONV = 4
GDN_QKV = 3 * GDN_HEADS * GDN_DK
SC_CONV = 3
N_MEM = 256
XA_HEADS = 4
XA_DIM = D_MODEL // XA_HEADS
CHUNK = 64
EPS = 1e-6
CONV_PAD = 8

VMEM_LIMIT_BYTES = 56 * 1024 * 1024


def _rms(x, g):
    return x * lax.rsqrt(jnp.mean(x * x, axis=-1, keepdims=True) + EPS) * g


def _dot(a, b):
    return jnp.dot(a, b, preferred_element_type=F32)


def _dot_nt(a, b):
    return lax.dot_general(a, b, (((1,), (1,)), ((), ())), preferred_element_type=F32)


def _dot_tn(a, b):
    return lax.dot_general(a, b, (((0,), (0,)), ((), ())), preferred_element_type=F32)


def _silu(x):
    return x * jax.nn.sigmoid(x)


def _swiglu_half_step(x, gpre, w_in, w_out, gpost):
    h = _rms(x, gpre).astype(BF16)
    gu = _dot(h, w_in)
    a = (_silu(gu[:, :D_FF]) * gu[:, D_FF:]).astype(BF16)
    y = _dot(a, w_out)
    return x + 0.5 * _rms(y, gpost)


def _ffn_kernel(x_ref, gpre_ref, win_ref, wout_ref, gpost_ref, o_ref):
    o_ref[...] = _swiglu_half_step(x_ref[...], gpre_ref[...], win_ref[...], wout_ref[...], gpost_ref[...])


def _memkv_kernel(m_ref, g_ref, w_ref, k_ref, v_ref):
    h = _rms(m_ref[...], g_ref[...]).astype(BF16)
    kv = _dot(h, w_ref[...])
    k_ref[...] = kv[:, :D_MODEL]
    v_ref[...] = kv[:, D_MODEL:]


def _xattn_ffn_kernel(x_ref, mk_ref, mv_ref, gxpre_ref, wq_ref, wo_ref, gxpost_ref,
                      gpre_ref, win_ref, wout_ref, gpost_ref, o_ref):
    x = x_ref[...]
    h = _rms(x, gxpre_ref[...]).astype(BF16)
    q = _dot(h, wq_ref[...]).astype(BF16)
    mk = mk_ref[...].astype(BF16)
    mv = mv_ref[...].astype(BF16)
    heads = []
    for hd in range(XA_HEADS):
        hs = slice(hd * XA_DIM, (hd + 1) * XA_DIM)
        s = _dot_nt(q[:, hs], mk[:, hs]) * (XA_DIM ** -0.5)
        e = jnp.exp(s - jnp.max(s, axis=-1, keepdims=True))
        p = (e * (1.0 / jnp.sum(e, axis=-1, keepdims=True))).astype(BF16)
        heads.append(_dot(p, mv[:, hs]))
    o = jnp.concatenate(heads, axis=-1).astype(BF16)
    x = x + _rms(_dot(o, wo_ref[...]), gxpost_ref[...])
    o_ref[...] = _swiglu_half_step(x, gpre_ref[...], win_ref[...], wout_ref[...], gpost_ref[...])


def _split3(x):
    hi = x.astype(BF16)
    r1 = x - hi.astype(F32)
    mid = r1.astype(BF16)
    lo = (r1 - mid.astype(F32)).astype(BF16)
    return hi, mid, lo


def _mixer_kernel(x_ref, s0_ref, gb0_ref, sb0_ref,
                  npre_ref, wqkv_ref, wz_ref, wa_ref, wb_ref, wsc_ref, wg_ref,
                  wconv_ref, alog_ref, dtb_ref, gnorm_ref, wgo_ref, wscc_ref, wsco_ref, wmo_ref, npost_ref,
                  y_ref, s_ref, gb_ref, sb_ref,
                  ext_ref, ext2_ref, q_s, k_s, v_s, gc_s, be_s, o_s, *, tm):
    t = pl.program_id(1)
    nchunk = tm // CHUNK

    @pl.when(t == 0)
    def _():
        s_ref[...] = s0_ref[...]
        ext_ref[CONV_PAD - (GDN_CONV - 1):CONV_PAD, :] = gb0_ref[...]
        ext2_ref[CONV_PAD - (SC_CONV - 1):CONV_PAD, :] = sb0_ref[...]

    x = x_ref[...]
    h = _rms(x, npre_ref[...]).astype(BF16)

    ext_ref[CONV_PAD:CONV_PAD + tm, :] = _dot(h, wqkv_ref[...])
    wc = wconv_ref[...]
    conv = 0.0
    for i in range(GDN_CONV):
        r0 = CONV_PAD - (GDN_CONV - 1) + i
        conv = conv + wc[i:i + 1, :] * ext_ref[r0:r0 + tm, :]
    tail = ext_ref[CONV_PAD + tm - (GDN_CONV - 1):CONV_PAD + tm, :]
    gb_ref[...] = tail
    ext_ref[CONV_PAD - (GDN_CONV - 1):CONV_PAD, :] = tail
    act = _silu(conv)
    nq = GDN_HEADS * GDN_DK
    for hd in range(GDN_HEADS):
        hs = slice(hd * GDN_DK, (hd + 1) * GDN_DK)
        qh = act[:, hd * GDN_DK:(hd + 1) * GDN_DK]
        kh = act[:, nq + hd * GDN_DK:nq + (hd + 1) * GDN_DK]
        q_s[:, hs] = qh * lax.rsqrt(jnp.sum(qh * qh, axis=-1, keepdims=True) + EPS) * (GDN_DK ** -0.5)
        k_s[:, hs] = kh * lax.rsqrt(jnp.sum(kh * kh, axis=-1, keepdims=True) + EPS)
    v_s[...] = act[:, 2 * nq:]

    a = _dot(h, wa_ref[...]) + dtb_ref[...]
    softplus = jnp.maximum(a, 0.0) + jnp.log1p(jnp.exp(-jnp.abs(a)))
    g = -jnp.exp(alog_ref[...]) * softplus
    ri = lax.broadcasted_iota(jnp.int32, (tm, tm), 0)
    ci = lax.broadcasted_iota(jnp.int32, (tm, tm), 1)
    same_chunk = (ri // CHUNK) == (ci // CHUNK)
    tri = jnp.where((ci <= ri) & same_chunk, 1.0, 0.0).astype(BF16)
    g_hi, g_mid, g_lo = _split3(g)
    gc_s[...] = _dot(tri, g_hi) + _dot(tri, g_mid) + _dot(tri, g_lo)
    be_s[...] = jax.nn.sigmoid(_dot(h, wb_ref[...]))

    li = lax.broadcasted_iota(jnp.int32, (CHUNK, CHUNK), 0)
    mi = lax.broadcasted_iota(jnp.int32, (CHUNK, CHUNK), 1)
    lower_incl = li >= mi
    lower_strict = li > mi
    eye = jnp.where(li == mi, 1.0, 0.0)

    def chunk_body(c, carry):
        r0 = pl.multiple_of(c * CHUNK, CHUNK)
        rows = pl.ds(r0, CHUNK)
        for hd in range(GDN_HEADS):
            hs = slice(hd * GDN_DK, (hd + 1) * GDN_DK)
            q = q_s[rows, hs]
            k = k_s[rows, hs]
            v = v_s[rows, hs]
            gcb = gc_s[rows, hs]
            bb = be_s[rows, hs]
            gc_row = gcb.T[:CHUNK, :]
            decay = jnp.exp(jnp.where(lower_incl, gcb[:, :CHUNK] - gc_row, -jnp.inf))
            e_gc = jnp.exp(gcb)
            g_end = gcb[CHUNK - 1:CHUNK, :]
            kb = k * bb
            k16 = k.astype(BF16)
            kkqk = _dot_nt(jnp.concatenate([kb, q], axis=0).astype(BF16), k16)
            a_mat = jnp.where(lower_strict, kkqk[:CHUNK] * decay, 0.0)
            qk = kkqk[CHUNK:] * decay
            xm = -a_mat
            tinv = eye + xm
            pw = xm
            for _ in range(5):
                pw16 = pw.astype(BF16)
                pw = _dot(pw16, pw16)
                tinv = tinv + _dot(tinv.astype(BF16), pw.astype(BF16))
            rhs = jnp.concatenate([v * bb, kb * e_gc], axis=1).astype(BF16)
            sol = _dot(tinv.astype(BF16), rhs)
            st = s_ref[hd]
            ws_qs = _dot(jnp.concatenate([sol[:, GDN_DK:], q * e_gc], axis=0).astype(BF16), st.astype(BF16))
            u = sol[:, :GDN_DK] - ws_qs[:CHUNK]
            u16 = u.astype(BF16)
            o_s[rows, hs] = ws_qs[CHUNK:] + _dot(qk.astype(BF16), u16)
            s_ref[hd] = st * jnp.exp(g_end) + _dot_tn((k * jnp.exp(g_end - gcb)).astype(BF16), u16)
        return carry

    lax.fori_loop(0, nchunk, chunk_body, 0)

    z = _dot(h, wz_ref[...])
    gn = gnorm_ref[...]
    og = []
    for hd in range(GDN_HEADS):
        hs = slice(hd * GDN_DK, (hd + 1) * GDN_DK)
        og.append((_rms(o_s[:, hs], gn) * _silu(z[:, hs])).astype(BF16))
    y_a = _dot(jnp.concatenate(og, axis=-1), wgo_ref[...])

    sc = _dot(h, wsc_ref[...])
    ext2_ref[CONV_PAD:CONV_PAD + tm, :] = sc[:, D_MODEL:2 * D_MODEL] * sc[:, 2 * D_MODEL:]
    w3 = wscc_ref[...]
    u_sc = 0.0
    for i in range(SC_CONV):
        r0 = CONV_PAD - (SC_CONV - 1) + i
        u_sc = u_sc + w3[i:i + 1, :] * ext2_ref[r0:r0 + tm, :]
    tail2 = ext2_ref[CONV_PAD + tm - (SC_CONV - 1):CONV_PAD + tm, :]
    sb_ref[...] = tail2
    ext2_ref[CONV_PAD - (SC_CONV - 1):CONV_PAD, :] = tail2
    y_b = _dot((sc[:, :D_MODEL] * u_sc).astype(BF16), wsco_ref[...])

    gg = _dot(h, wg_ref[...])
    y = jax.nn.sigmoid(gg[:, :D_MODEL]) * y_a + jax.nn.sigmoid(gg[:, D_MODEL:]) * y_b
    m = _dot(y.astype(BF16), wmo_ref[...])
    y_ref[...] = x + _rms(m, npost_ref[...])


def _const_spec(shape):
    zeros = (0,) * len(shape)
    return pl.BlockSpec(shape, lambda *_: zeros, pipeline_mode=pl.Buffered(1))


def _row_tile(n, pref):
    tm = min(n, pref)
    assert n % tm == 0, (n, tm)
    return tm


def _ffn_call(x2d, gpre, w_in, w_out, gpost):
    n = x2d.shape[0]
    tm = _row_tile(n, 256)
    row = pl.BlockSpec((tm, D_MODEL), lambda i: (i, 0))
    return pl.pallas_call(
        _ffn_kernel,
        grid=(n // tm,),
        in_specs=[row, _const_spec(gpre.shape), _const_spec(w_in.shape), _const_spec(w_out.shape),
                  _const_spec(gpost.shape)],
        out_specs=row,
        out_shape=jax.ShapeDtypeStruct(x2d.shape, F32),
        compiler_params=pltpu.CompilerParams(dimension_semantics=("arbitrary",),
                                             vmem_limit_bytes=VMEM_LIMIT_BYTES),
        name="ffn_half_step",
    )(x2d, gpre, w_in, w_out, gpost)


def _memkv_call(mem2d, g, w):
    n = mem2d.shape[0]
    tm = _row_tile(n, 256)
    row = pl.BlockSpec((tm, D_MODEL), lambda i: (i, 0))
    return pl.pallas_call(
        _memkv_kernel,
        grid=(n // tm,),
        in_specs=[row, _const_spec(g.shape), _const_spec(w.shape)],
        out_specs=[row, row],
        out_shape=[jax.ShapeDtypeStruct(mem2d.shape, F32)] * 2,
        compiler_params=pltpu.CompilerParams(dimension_semantics=("arbitrary",),
                                             vmem_limit_bytes=VMEM_LIMIT_BYTES),
        name="memory_kv",
    )(mem2d, g, w)


def _xattn_ffn_call(x, mk, mv, gxpre, wq, wo, gxpost, gpre, w_in, w_out, gpost):
    b, t, _ = x.shape
    tm = _row_tile(t, 256)
    row = pl.BlockSpec((None, tm, D_MODEL), lambda i, j: (i, j, 0))
    mem = pl.BlockSpec((None, N_MEM, D_MODEL), lambda i, j: (i, 0, 0))
    consts = (gxpre, wq, wo, gxpost, gpre, w_in, w_out, gpost)
    return pl.pallas_call(
        _xattn_ffn_kernel,
        grid=(b, t // tm),
        in_specs=[row, mem, mem] + [_const_spec(c.shape) for c in consts],
        out_specs=row,
        out_shape=jax.ShapeDtypeStruct(x.shape, F32),
        compiler_params=pltpu.CompilerParams(dimension_semantics=("arbitrary", "arbitrary"),
                                             vmem_limit_bytes=VMEM_LIMIT_BYTES),
        name="xattn_ffn",
    )(x, mk, mv, *consts)


def _mixer_call(x, s0, gb0, sb0, consts):
    b, t, _ = x.shape
    tm = _row_tile(t, 128)
    assert tm % CHUNK == 0, tm
    row = pl.BlockSpec((None, tm, D_MODEL), lambda i, j: (i, j, 0))
    s_spec = pl.BlockSpec((None, GDN_HEADS, GDN_DK, GDN_DK), lambda i, j: (i, 0, 0, 0))
    gb_spec = pl.BlockSpec((None, GDN_CONV - 1, GDN_QKV), lambda i, j: (i, 0, 0))
    sb_spec = pl.BlockSpec((None, SC_CONV - 1, D_MODEL), lambda i, j: (i, 0, 0))
    tile = lambda w: pltpu.VMEM((tm, w), F32)
    return pl.pallas_call(
        functools.partial(_mixer_kernel, tm=tm),
        grid=(b, t // tm),
        in_specs=[row, s_spec, gb_spec, sb_spec] + [_const_spec(c.shape) for c in consts],
        out_specs=[row, s_spec, gb_spec, sb_spec],
        out_shape=[jax.ShapeDtypeStruct(x.shape, F32), jax.ShapeDtypeStruct(s0.shape, F32),
                   jax.ShapeDtypeStruct(gb0.shape, F32), jax.ShapeDtypeStruct(sb0.shape, F32)],
        scratch_shapes=[pltpu.VMEM((CONV_PAD + tm, GDN_QKV), F32), pltpu.VMEM((CONV_PAD + tm, D_MODEL), F32),
                        tile(D_MODEL), tile(D_MODEL), tile(D_MODEL), tile(D_MODEL), tile(D_MODEL), tile(D_MODEL)],
        compiler_params=pltpu.CompilerParams(dimension_semantics=("arbitrary", "arbitrary"),
                                             vmem_limit_bytes=VMEM_LIMIT_BYTES),
        name="token_mixer",
    )(x, s0, gb0, sb0, *consts)


def _per_head_lanes(v):
    return jnp.repeat(v.astype(F32), GDN_DK)[None, :]


def _layer(x, mk, mv, s0, gb0, sb0, p):
    b, t, _ = x.shape
    x = _ffn_call(x.reshape(b * t, D_MODEL), p['n_ffn1_pre'], p['w_ffn1_in'], p['w_ffn1_out'],
                  p['n_ffn1_post']).reshape(b, t, D_MODEL)
    x, s_new, gb_new, sb_new = _mixer_call(x, s0, gb0, sb0, p['mixer_consts'])
    x = _xattn_ffn_call(x, mk.reshape(b, N_MEM, D_MODEL), mv.reshape(b, N_MEM, D_MODEL),
                        p['n_x_pre'], p['w_xq'], p['w_xo'], p['n_x_post'],
                        p['n_ffn2_pre'], p['w_ffn2_in'], p['w_ffn2_out'], p['n_ffn2_post'])
    return x, s_new, gb_new, sb_new


def _prep_params(raw):
    p = {}
    for name, arr in raw.items():
        a = arr[0]
        if name.startswith('w_') and name not in ('w_gdn_conv', 'w_sc_conv', 'w_mix_in'):
            p[name] = a.astype(BF16)
        elif name.startswith('n_') or name == 'gdn_norm':
            p[name] = a[None, :]
    w = raw['w_mix_in'][0]
    nq = GDN_HEADS * GDN_DK
    c_z, c_a, c_b, c_sc = 3 * nq, 4 * nq, 4 * nq + GDN_HEADS, 4 * nq + 2 * GDN_HEADS
    c_g = c_sc + 3 * D_MODEL
    w_a = jnp.repeat(w[:, c_a:c_b], GDN_DK, axis=1)
    w_b = jnp.repeat(w[:, c_b:c_sc], GDN_DK, axis=1)
    p['mixer_consts'] = (
        p['n_mix_pre'], w[:, :c_z].astype(BF16), w[:, c_z:c_a].astype(BF16), w_a.astype(BF16), w_b.astype(BF16),
        w[:, c_sc:c_g].astype(BF16), w[:, c_g:].astype(BF16),
        raw['w_gdn_conv'][0], _per_head_lanes(raw['gdn_a_log'][0]), _per_head_lanes(raw['gdn_dt_bias'][0]),
        p['gdn_norm'], p['w_gdn_out'], raw['w_sc_conv'][0], p['w_sc_out'], p['w_mix_out'], p['n_mix_post'])
    return p


def kernel(x_prompt, x_sample, mem_prompt, cache_mem_k, cache_mem_v, state_gdn, state_gdn_conv, state_shortconv, n_ffn1_pre, w_ffn1_in, w_ffn1_out, n_ffn1_post, n_mix_pre, w_mix_in, w_gdn_conv, gdn_a_log, gdn_dt_bias, gdn_norm, w_gdn_out, w_sc_conv, w_sc_out, w_mix_out, n_mix_post, n_x_pre, n_mem, w_xq, w_xkv, w_xo, n_x_post, n_ffn2_pre, w_ffn2_in, w_ffn2_out, n_ffn2_post):
    raw = dict(n_ffn1_pre=n_ffn1_pre, w_ffn1_in=w_ffn1_in, w_ffn1_out=w_ffn1_out, n_ffn1_post=n_ffn1_post,
               n_mix_pre=n_mix_pre, w_mix_in=w_mix_in, w_gdn_conv=w_gdn_conv, gdn_a_log=gdn_a_log,
               gdn_dt_bias=gdn_dt_bias, gdn_norm=gdn_norm, w_gdn_out=w_gdn_out, w_sc_conv=w_sc_conv,
               w_sc_out=w_sc_out, w_mix_out=w_mix_out, n_mix_post=n_mix_post, n_x_pre=n_x_pre,
               n_mem=n_mem, w_xq=w_xq, w_xkv=w_xkv, w_xo=w_xo, n_x_post=n_x_post,
               n_ffn2_pre=n_ffn2_pre, w_ffn2_in=w_ffn2_in, w_ffn2_out=w_ffn2_out, n_ffn2_post=n_ffn2_post)
    assert w_mix_in.shape[0] == 1, "single-layer model"
    p = _prep_params(raw)
    bp = x_prompt.shape[0]
    mk_p, mv_p = _memkv_call(mem_prompt.reshape(bp * N_MEM, D_MODEL), p['n_mem'], p['w_xkv'])
    mk_p = mk_p.reshape(bp, N_MEM, XA_HEADS, XA_DIM)
    mv_p = mv_p.reshape(bp, N_MEM, XA_HEADS, XA_DIM)
    s0 = jnp.zeros((bp, GDN_HEADS, GDN_DK, GDN_DK), F32)
    gb0 = jnp.zeros((bp, GDN_CONV - 1, GDN_QKV), F32)
    sb0 = jnp.zeros((bp, SC_CONV - 1, D_MODEL), F32)
    yp, sp, gbp, sbp = _layer(x_prompt, mk_p, mv_p, s0, gb0, sb0, p)
    ys, ss, gbs, sbs = _layer(x_sample, cache_mem_k[0], cache_mem_v[0], state_gdn[0], state_gdn_conv[0],
                              state_shortconv[0], p)
    return (yp, ys, sp[None], gbp[None], sbp[None], mk_p[None], mv_p[None], ss[None], gbs[None], sbs[None])
```

```python
import functools

import jax
import jax.numpy as jnp
from jax import lax
from jax.experimental import pallas as pl
from jax.experimental.pallas import tpu as pltpu

F32 = jnp.float32
BF16 = jnp.bfloat16

D_MODEL = 1024
D_FF = 2816
GDN_HEADS = 8
GDN_DK = 128
GDN_CONV = 4
GDN_QKV = 3 * GDN_HEADS * GDN_DK
SC_CONV = 3
N_MEM = 256
XA_HEADS = 4
XA_DIM = D_MODEL // XA_HEADS
CHUNK = 64
EPS = 1e-6
CONV_PAD = 8

VMEM_LIMIT_BYTES = 56 * 1024 * 1024


def _rms(x, g):
    return x * lax.rsqrt(jnp.mean(x * x, axis=-1, keepdims=True) + EPS) * g


def _dot(a, b):
    return jnp.dot(a, b, preferred_element_type=F32)


def _dot_nt(a, b):
    return lax.dot_general(a, b, (((1,), (1,)), ((), ())), preferred_element_type=F32)


def _dot_tn(a, b):
    return lax.dot_general(a, b, (((0,), (0,)), ((), ())), preferred_element_type=F32)


def _silu(x):
    return x * jax.nn.sigmoid(x)


def _swiglu_half_step(x, gpre, w_in, w_out, gpost):
    h = _rms(x, gpre).astype(BF16)
    gu = _dot(h, w_in)
    a = (_silu(gu[:, :D_FF]) * gu[:, D_FF:]).astype(BF16)
    y = _dot(a, w_out)
    return x + 0.5 * _rms(y, gpost)


def _ffn_kernel(x_ref, gpre_ref, win_ref, wout_ref, gpost_ref, o_ref):
    o_ref[...] = _swiglu_half_step(x_ref[...], gpre_ref[...], win_ref[...], wout_ref[...], gpost_ref[...])


def _memkv_kernel(m_ref, g_ref, w_ref, k_ref, v_ref):
    h = _rms(m_ref[...], g_ref[...]).astype(BF16)
    kv = _dot(h, w_ref[...])
    k_ref[...] = kv[:, :D_MODEL]
    v_ref[...] = kv[:, D_MODEL:]


def _xattn_ffn_kernel(x_ref, mk_ref, mv_ref, gxpre_ref, wq_ref, wo_ref, gxpost_ref,
                      gpre_ref, win_ref, wout_ref, gpost_ref, o_ref):
    x = x_ref[...]
    h = _rms(x, gxpre_ref[...]).astype(BF16)
    q = _dot(h, wq_ref[...]).astype(BF16)
    mk = mk_ref[...].astype(BF16)
    mv = mv_ref[...].astype(BF16)
    heads = []
    for hd in range(XA_HEADS):
        hs = slice(hd * XA_DIM, (hd + 1) * XA_DIM)
        s = _dot_nt(q[:, hs], mk[:, hs]) * (XA_DIM ** -0.5)
        e = jnp.exp(s - jnp.max(s, axis=-1, keepdims=True))
        p = (e * (1.0 / jnp.sum(e, axis=-1, keepdims=True))).astype(BF16)
        heads.append(_dot(p, mv[:, hs]))
    o = jnp.concatenate(heads, axis=-1).astype(BF16)
    x = x + _rms(_dot(o, wo_ref[...]), gxpost_ref[...])
    o_ref[...] = _swiglu_half_step(x, gpre_ref[...], win_ref[...], wout_ref[...], gpost_ref[...])


def _split3(x):
    hi = x.astype(BF16)
    r1 = x - hi.astype(F32)
    mid = r1.astype(BF16)
    lo = (r1 - mid.astype(F32)).astype(BF16)
    return hi, mid, lo


def _mixer_kernel(x_ref, s0_ref, gb0_ref, sb0_ref,
                  npre_ref, wqkv_ref, wz_ref, wa_ref, wb_ref, wsc_ref, wg_ref,
                  wconv_ref, alog_ref, dtb_ref, gnorm_ref, wgo_ref, wscc_ref, wsco_ref, wmo_ref, npost_ref,
                  y_ref, s_ref, gb_ref, sb_ref,
                  ext_ref, ext2_ref, q_s, k_s, v_s, gc_s, be_s, o_s, *, tm):
    t = pl.program_id(1)
    nchunk = tm // CHUNK

    @pl.when(t == 0)
    def _():
        s_ref[...] = s0_ref[...]
        ext_ref[CONV_PAD - (GDN_CONV - 1):CONV_PAD, :] = gb0_ref[...]
        ext2_ref[CONV_PAD - (SC_CONV - 1):CONV_PAD, :] = sb0_ref[...]

    x = x_ref[...]
    h = _rms(x, npre_ref[...]).astype(BF16)

    ext_ref[CONV_PAD:CONV_PAD + tm, :] = _dot(h, wqkv_ref[...])
    wc = wconv_ref[...]
    conv = 0.0
    for i in range(GDN_CONV):
        r0 = CONV_PAD - (GDN_CONV - 1) + i
        conv = conv + wc[i:i + 1, :] * ext_ref[r0:r0 + tm, :]
    tail = ext_ref[CONV_PAD + tm - (GDN_CONV - 1):CONV_PAD + tm, :]
    gb_ref[...] = tail
    ext_ref[CONV_PAD - (GDN_CONV - 1):CONV_PAD, :] = tail
    act = _silu(conv)
    nq = GDN_HEADS * GDN_DK
    for hd in range(GDN_HEADS):
        hs = slice(hd * GDN_DK, (hd + 1) * GDN_DK)
        qh = act[:, hd * GDN_DK:(hd + 1) * GDN_DK]
        kh = act[:, nq + hd * GDN_DK:nq + (hd + 1) * GDN_DK]
        q_s[:, hs] = qh * lax.rsqrt(jnp.sum(qh * qh, axis=-1, keepdims=True) + EPS) * (GDN_DK ** -0.5)
        k_s[:, hs] = kh * lax.rsqrt(jnp.sum(kh * kh, axis=-1, keepdims=True) + EPS)
    v_s[...] = act[:, 2 * nq:]

    a = _dot(h, wa_ref[...]) + dtb_ref[...]
    softplus = jnp.maximum(a, 0.0) + jnp.log1p(jnp.exp(-jnp.abs(a)))
    g = -jnp.exp(alog_ref[...]) * softplus
    ri = lax.broadcasted_iota(jnp.int32, (tm, tm), 0)
    ci = lax.broadcasted_iota(jnp.int32, (tm, tm), 1)
    same_chunk = (ri // CHUNK) == (ci // CHUNK)
    tri = jnp.where((ci <= ri) & same_chunk, 1.0, 0.0).astype(BF16)
    g_hi, g_mid, g_lo = _split3(g)
    gc_s[...] = _dot(tri, g_hi) + _dot(tri, g_mid) + _dot(tri, g_lo)
    be_s[...] = jax.nn.sigmoid(_dot(h, wb_ref[...]))

    li = lax.broadcasted_iota(jnp.int32, (CHUNK, CHUNK), 0)
    mi = lax.broadcasted_iota(jnp.int32, (CHUNK, CHUNK), 1)
    lower_incl = li >= mi
    lower_strict = li > mi
    eye = jnp.where(li == mi, 1.0, 0.0)

    head_lanes = [slice(hd * GDN_DK, (hd + 1) * GDN_DK) for hd in range(GDN_HEADS)]
    pairs = [(slice(c * CHUNK, (c + 1) * CHUNK), hs) for c in range(nchunk) for hs in head_lanes]
    q = [q_s[r, hs] for r, hs in pairs]
    k = [k_s[r, hs] for r, hs in pairs]
    gcb = [gc_s[r, hs] for r, hs in pairs]
    bb = [be_s[r, hs] for r, hs in pairs]
    kb = [ki * bi for ki, bi in zip(k, bb)]
    kkqk = [_dot_nt(jnp.concatenate([kbi, qi], axis=0).astype(BF16), ki.astype(BF16))
            for kbi, qi, ki in zip(kb, q, k)]
    decay = [jnp.exp(jnp.where(lower_incl, g[:, :CHUNK] - g.T[:CHUNK, :], -jnp.inf)) for g in gcb]
    a_mat = [jnp.where(lower_strict, m[:CHUNK] * d, 0.0) for m, d in zip(kkqk, decay)]
    qk = [(m[CHUNK:] * d).astype(BF16) for m, d in zip(kkqk, decay)]
    tinv = [eye - a for a in a_mat]
    pw = [a.astype(BF16) for a in a_mat]
    for _ in range(5):
        pw = [_dot(p, p) for p in pw]
        pw = [p.astype(BF16) for p in pw]
        tinv = [ti + _dot(ti.astype(BF16), p) for ti, p in zip(tinv, pw)]
    e_gc = [jnp.exp(g) for g in gcb]
    rhs = [jnp.concatenate([v_s[r, hs] * bi, kbi * ei], axis=1).astype(BF16)
           for (r, hs), bi, kbi, ei in zip(pairs, bb, kb, e_gc)]
    sol = [_dot(ti.astype(BF16), ri) for ti, ri in zip(tinv, rhs)]
    g_end = [g[CHUNK - 1:CHUNK, :] for g in gcb]
    w_qe = [jnp.concatenate([so[:, GDN_DK:], qi * ei], axis=0).astype(BF16) for so, qi, ei in zip(sol, q, e_gc)]
    k_rem = [(ki * jnp.exp(ge - g)).astype(BF16) for ki, ge, g in zip(k, g_end, gcb)]
    e_end = [jnp.exp(ge) for ge in g_end]
    for c in range(nchunk):
        sel = range(c * GDN_HEADS, (c + 1) * GDN_HEADS)
        st = [s_ref[hd] for hd in range(GDN_HEADS)]
        ws_qs = [_dot(w_qe[i], st[hd].astype(BF16)) for hd, i in enumerate(sel)]
        u16 = [(sol[i][:, :GDN_DK] - ws_qs[hd][:CHUNK]).astype(BF16) for hd, i in enumerate(sel)]
        for hd, i in enumerate(sel):
            r, hs = pairs[i]
            o_s[r, hs] = ws_qs[hd][CHUNK:] + _dot(qk[i], u16[hd])
        for hd, i in enumerate(sel):
            s_ref[hd] = st[hd] * e_end[i] + _dot_tn(k_rem[i], u16[hd])

    z = _dot(h, wz_ref[...])
    gn = gnorm_ref[...]
    og = []
    for hd in range(GDN_HEADS):
        hs = slice(hd * GDN_DK, (hd + 1) * GDN_DK)
        og.append((_rms(o_s[:, hs], gn) * _silu(z[:, hs])).astype(BF16))
    y_a = _dot(jnp.concatenate(og, axis=-1), wgo_ref[...])

    sc = _dot(h, wsc_ref[...])
    ext2_ref[CONV_PAD:CONV_PAD + tm, :] = sc[:, D_MODEL:2 * D_MODEL] * sc[:, 2 * D_MODEL:]
    w3 = wscc_ref[...]
    u_sc = 0.0
    for i in range(SC_CONV):
        r0 = CONV_PAD - (SC_CONV - 1) + i
        u_sc = u_sc + w3[i:i + 1, :] * ext2_ref[r0:r0 + tm, :]
    tail2 = ext2_ref[CONV_PAD + tm - (SC_CONV - 1):CONV_PAD + tm, :]
    sb_ref[...] = tail2
    ext2_ref[CONV_PAD - (SC_CONV - 1):CONV_PAD, :] = tail2
    y_b = _dot((sc[:, :D_MODEL] * u_sc).astype(BF16), wsco_ref[...])

    gg = _dot(h, wg_ref[...])
    y = jax.nn.sigmoid(gg[:, :D_MODEL]) * y_a + jax.nn.sigmoid(gg[:, D_MODEL:]) * y_b
    m = _dot(y.astype(BF16), wmo_ref[...])
    y_ref[...] = x + _rms(m, npost_ref[...])


def _const_spec(shape):
    zeros = (0,) * len(shape)
    return pl.BlockSpec(shape, lambda *_: zeros, pipeline_mode=pl.Buffered(1))


def _row_tile(n, pref):
    tm = min(n, pref)
    assert n % tm == 0, (n, tm)
    return tm


def _ffn_call(x2d, gpre, w_in, w_out, gpost):
    n = x2d.shape[0]
    tm = _row_tile(n, 256)
    row = pl.BlockSpec((tm, D_MODEL), lambda i: (i, 0))
    return pl.pallas_call(
        _ffn_kernel,
        grid=(n // tm,),
        in_specs=[row, _const_spec(gpre.shape), _const_spec(w_in.shape), _const_spec(w_out.shape),
                  _const_spec(gpost.shape)],
        out_specs=row,
        out_shape=jax.ShapeDtypeStruct(x2d.shape, F32),
        compiler_params=pltpu.CompilerParams(dimension_semantics=("arbitrary",),
                                             vmem_limit_bytes=VMEM_LIMIT_BYTES),
        name="ffn_half_step",
    )(x2d, gpre, w_in, w_out, gpost)


def _memkv_call(mem2d, g, w):
    n = mem2d.shape[0]
    tm = _row_tile(n, 256)
    row = pl.BlockSpec((tm, D_MODEL), lambda i: (i, 0))
    return pl.pallas_call(
        _memkv_kernel,
        grid=(n // tm,),
        in_specs=[row, _const_spec(g.shape), _const_spec(w.shape)],
        out_specs=[row, row],
        out_shape=[jax.ShapeDtypeStruct(mem2d.shape, F32)] * 2,
        compiler_params=pltpu.CompilerParams(dimension_semantics=("arbitrary",),
                                             vmem_limit_bytes=VMEM_LIMIT_BYTES),
        name="memory_kv",
    )(mem2d, g, w)


def _xattn_ffn_call(x, mk, mv, gxpre, wq, wo, gxpost, gpre, w_in, w_out, gpost):
    b, t, _ = x.shape
    tm = _row_tile(t, 256)
    row = pl.BlockSpec((None, tm, D_MODEL), lambda i, j: (i, j, 0))
    mem = pl.BlockSpec((None, N_MEM, D_MODEL), lambda i, j: (i, 0, 0))
    consts = (gxpre, wq, wo, gxpost, gpre, w_in, w_out, gpost)
    return pl.pallas_call(
        _xattn_ffn_kernel,
        grid=(b, t // tm),
        in_specs=[row, mem, mem] + [_const_spec(c.shape) for c in consts],
        out_specs=row,
        out_shape=jax.ShapeDtypeStruct(x.shape, F32),
        compiler_params=pltpu.CompilerParams(dimension_semantics=("arbitrary", "arbitrary"),
                                             vmem_limit_bytes=VMEM_LIMIT_BYTES),
        name="xattn_ffn",
    )(x, mk, mv, *consts)


def _mixer_call(x, s0, gb0, sb0, consts):
    b, t, _ = x.shape
    tm = _row_tile(t, 128)
    assert tm % CHUNK == 0, tm
    row = pl.BlockSpec((None, tm, D_MODEL), lambda i, j: (i, j, 0))
    s_spec = pl.BlockSpec((None, GDN_HEADS, GDN_DK, GDN_DK), lambda i, j: (i, 0, 0, 0))
    gb_spec = pl.BlockSpec((None, GDN_CONV - 1, GDN_QKV), lambda i, j: (i, 0, 0))
    sb_spec = pl.BlockSpec((None, SC_CONV - 1, D_MODEL), lambda i, j: (i, 0, 0))
    tile = lambda w: pltpu.VMEM((tm, w), F32)
    return pl.pallas_call(
        functools.partial(_mixer_kernel, tm=tm),
        grid=(b, t // tm),
        in_specs=[row, s_spec, gb_spec, sb_spec] + [_const_spec(c.shape) for c in consts],
        out_specs=[row, s_spec, gb_spec, sb_spec],
        out_shape=[jax.ShapeDtypeStruct(x.shape, F32), jax.ShapeDtypeStruct(s0.shape, F32),
                   jax.ShapeDtypeStruct(gb0.shape, F32), jax.ShapeDtypeStruct(sb0.shape, F32)],
        scratch_shapes=[pltpu.VMEM((CONV_PAD + tm, GDN_QKV), F32), pltpu.VMEM((CONV_PAD + tm, D_MODEL), F32),
                        tile(D_MODEL), tile(D_MODEL), tile(D_MODEL), tile(D_MODEL), tile(D_MODEL), tile(D_MODEL)],
        compiler_params=pltpu.CompilerParams(dimension_semantics=("arbitrary", "arbitrary"),
                                             vmem_limit_bytes=VMEM_LIMIT_BYTES),
        name="token_mixer",
    )(x, s0, gb0, sb0, *consts)


def _per_head_lanes(v):
    return jnp.repeat(v.astype(F32), GDN_DK)[None, :]


def _layer(x, mk, mv, s0, gb0, sb0, p):
    b, t, _ = x.shape
    x = _ffn_call(x.reshape(b * t, D_MODEL), p['n_ffn1_pre'], p['w_ffn1_in'], p['w_ffn1_out'],
                  p['n_ffn1_post']).reshape(b, t, D_MODEL)
    x, s_new, gb_new, sb_new = _mixer_call(x, s0, gb0, sb0, p['mixer_consts'])
    x = _xattn_ffn_call(x, mk.reshape(b, N_MEM, D_MODEL), mv.reshape(b, N_MEM, D_MODEL),
                        p['n_x_pre'], p['w_xq'], p['w_xo'], p['n_x_post'],
                        p['n_ffn2_pre'], p['w_ffn2_in'], p['w_ffn2_out'], p['n_ffn2_post'])
    return x, s_new, gb_new, sb_new


def _prep_params(raw):
    p = {}
    for name, arr in raw.items():
        a = arr[0]
        if name.startswith('w_') and name not in ('w_gdn_conv', 'w_sc_conv', 'w_mix_in'):
            p[name] = a.astype(BF16)
        elif name.startswith('n_') or name == 'gdn_norm':
            p[name] = a[None, :]
    w = raw['w_mix_in'][0]
    nq = GDN_HEADS * GDN_DK
    c_z, c_a, c_b, c_sc = 3 * nq, 4 * nq, 4 * nq + GDN_HEADS, 4 * nq + 2 * GDN_HEADS
    c_g = c_sc + 3 * D_MODEL
    w_a = jnp.repeat(w[:, c_a:c_b], GDN_DK, axis=1)
    w_b = jnp.repeat(w[:, c_b:c_sc], GDN_DK, axis=1)
    p['mixer_consts'] = (
        p['n_mix_pre'], w[:, :c_z].astype(BF16), w[:, c_z:c_a].astype(BF16), w_a.astype(BF16), w_b.astype(BF16),
        w[:, c_sc:c_g].astype(BF16), w[:, c_g:].astype(BF16),
        raw['w_gdn_conv'][0], _per_head_lanes(raw['gdn_a_log'][0]), _per_head_lanes(raw['gdn_dt_bias'][0]),
        p['gdn_norm'], p['w_gdn_out'], raw['w_sc_conv'][0], p['w_sc_out'], p['w_mix_out'], p['n_mix_post'])
    return p


def kernel(x_prompt, x_sample, mem_prompt, cache_mem_k, cache_mem_v, state_gdn, state_gdn_conv, state_shortconv, n_ffn1_pre, w_ffn1_in, w_ffn1_out, n_ffn1_post, n_mix_pre, w_mix_in, w_gdn_conv, gdn_a_log, gdn_dt_bias, gdn_norm, w_gdn_out, w_sc_conv, w_sc_out, w_mix_out, n_mix_post, n_x_pre, n_mem, w_xq, w_xkv, w_xo, n_x_post, n_ffn2_pre, w_ffn2_in, w_ffn2_out, n_ffn2_post):
    raw = dict(n_ffn1_pre=n_ffn1_pre, w_ffn1_in=w_ffn1_in, w_ffn1_out=w_ffn1_out, n_ffn1_post=n_ffn1_post,
               n_mix_pre=n_mix_pre, w_mix_in=w_mix_in, w_gdn_conv=w_gdn_conv, gdn_a_log=gdn_a_log,
               gdn_dt_bias=gdn_dt_bias, gdn_norm=gdn_norm, w_gdn_out=w_gdn_out, w_sc_conv=w_sc_conv,
               w_sc_out=w_sc_out, w_mix_out=w_mix_out, n_mix_post=n_mix_post, n_x_pre=n_x_pre,
               n_mem=n_mem, w_xq=w_xq, w_xkv=w_xkv, w_xo=w_xo, n_x_post=n_x_post,
               n_ffn2_pre=n_ffn2_pre, w_ffn2_in=w_ffn2_in, w_ffn2_out=w_ffn2_out, n_ffn2_post=n_ffn2_post)
    assert w_mix_in.shape[0] == 1, "single-layer model"
    p = _prep_params(raw)
    bp = x_prompt.shape[0]
    mk_p, mv_p = _memkv_call(mem_prompt.reshape(bp * N_MEM, D_MODEL), p['n_mem'], p['w_xkv'])
    mk_p = mk_p.reshape(bp, N_MEM, XA_HEADS, XA_DIM)
    mv_p = mv_p.reshape(bp, N_MEM, XA_HEADS, XA_DIM)
    s0 = jnp.zeros((bp, GDN_HEADS, GDN_DK, GDN_DK), F32)
    gb0 = jnp.zeros((bp, GDN_CONV - 1, GDN_QKV), F32)
    sb0 = jnp.zeros((bp, SC_CONV - 1, D_MODEL), F32)
    yp, sp, gbp, sbp = _layer(x_prompt, mk_p, mv_p, s0, gb0, sb0, p)
    ys, ss, gbs, sbs = _layer(x_sample, cache_mem_k[0], cache_mem_v[0], state_gdn[0], state_gdn_conv[0],
                              state_shortconv[0], p)
    return (yp, ys, sp[None], gbp[None], sbp[None], mk_p[None], mv_p[None], ss[None], gbs[None], sbs[None])
```

```python
import functools

import jax
import jax.numpy as jnp
from jax import lax
from jax.experimental import pallas as pl
from jax.experimental.pallas import tpu as pltpu

F32 = jnp.float32
BF16 = jnp.bfloat16

D_MODEL = 1024
D_FF = 2816
GDN_HEADS = 8
GDN_DK = 128
GDN_CONV = 4
GDN_QKV = 3 * GDN_HEADS * GDN_DK
SC_CONV = 3
N_MEM = 256
XA_HEADS = 4
XA_DIM = D_MODEL // XA_HEADS
CHUNK = 64
EPS = 1e-6
CONV_PAD = 8

VMEM_LIMIT_BYTES = 56 * 1024 * 1024


def _rms(x, g):
    return x * lax.rsqrt(jnp.mean(x * x, axis=-1, keepdims=True) + EPS) * g


def _dot(a, b):
    return jnp.dot(a, b, preferred_element_type=F32)


def _dot_nt(a, b):
    return lax.dot_general(a, b, (((1,), (1,)), ((), ())), preferred_element_type=F32)


def _dot_tn(a, b):
    return lax.dot_general(a, b, (((0,), (0,)), ((), ())), preferred_element_type=F32)


def _silu(x):
    return x * jax.nn.sigmoid(x)


def _swiglu_half_step(x, gpre, w_in, w_out, gpost):
    h = _rms(x, gpre).astype(BF16)
    gu = _dot(h, w_in)
    a = (_silu(gu[:, :D_FF]) * gu[:, D_FF:]).astype(BF16)
    y = _dot(a, w_out)
    return x + 0.5 * _rms(y, gpost)


def _ffn_kernel(x_ref, gpre_ref, win_ref, wout_ref, gpost_ref, o_ref):
    o_ref[...] = _swiglu_half_step(x_ref[...], gpre_ref[...], win_ref[...], wout_ref[...], gpost_ref[...])


def _memkv_kernel(m_ref, g_ref, w_ref, k_ref, v_ref):
    h = _rms(m_ref[...], g_ref[...]).astype(BF16)
    kv = _dot(h, w_ref[...])
    k_ref[...] = kv[:, :D_MODEL]
    v_ref[...] = kv[:, D_MODEL:]


def _xattn_ffn_kernel(x_ref, mk_ref, mv_ref, gxpre_ref, wq_ref, wo_ref, gxpost_ref,
                      gpre_ref, win_ref, wout_ref, gpost_ref, o_ref):
    x = x_ref[...]
    h = _rms(x, gxpre_ref[...]).astype(BF16)
    q = _dot(h, wq_ref[...]).astype(BF16)
    mk = mk_ref[...].astype(BF16)
    mv = mv_ref[...].astype(BF16)
    heads = []
    for hd in range(XA_HEADS):
        hs = slice(hd * XA_DIM, (hd + 1) * XA_DIM)
        s = _dot_nt(q[:, hs], mk[:, hs]) * (XA_DIM ** -0.5)
        e = jnp.exp(s - jnp.max(s, axis=-1, keepdims=True))
        p = (e * (1.0 / jnp.sum(e, axis=-1, keepdims=True))).astype(BF16)
        heads.append(_dot(p, mv[:, hs]))
    o = jnp.concatenate(heads, axis=-1).astype(BF16)
    x = x + _rms(_dot(o, wo_ref[...]), gxpost_ref[...])
    o_ref[...] = _swiglu_half_step(x, gpre_ref[...], win_ref[...], wout_ref[...], gpost_ref[...])


def _split3(x):
    hi = x.astype(BF16)
    r1 = x - hi.astype(F32)
    mid = r1.astype(BF16)
    lo = (r1 - mid.astype(F32)).astype(BF16)
    return hi, mid, lo


def _mixer_kernel(x_ref, s0_ref, gb0_ref, sb0_ref,
                  npre_ref, wqkv_ref, wz_ref, wab_ref, wsc_ref, wg_ref,
                  wconv_ref, alog_ref, dtb_ref, gnorm_ref, wgo_ref, wscc_ref, wsco_ref, wmo_ref, npost_ref,
                  y_ref, s_ref, gb_ref, sb_ref,
                  ext_ref, ext2_ref, q_s, k_s, v_s, gc_s, be_s, o_s, *, tm):
    t = pl.program_id(1)
    nchunk = tm // CHUNK

    @pl.when(t == 0)
    def _():
        s_ref[...] = s0_ref[...]
        ext_ref[CONV_PAD - (GDN_CONV - 1):CONV_PAD, :] = gb0_ref[...]
        ext2_ref[CONV_PAD - (SC_CONV - 1):CONV_PAD, :] = sb0_ref[...]

    x = x_ref[...]
    h = _rms(x, npre_ref[...]).astype(BF16)

    sc = _dot(h, wsc_ref[...])
    ext2_ref[CONV_PAD:CONV_PAD + tm, :] = sc[:, D_MODEL:2 * D_MODEL] * sc[:, 2 * D_MODEL:]
    w3 = wscc_ref[...]
    u_sc = 0.0
    for i in range(SC_CONV):
        r0 = CONV_PAD - (SC_CONV - 1) + i
        u_sc = u_sc + w3[i:i + 1, :] * ext2_ref[r0:r0 + tm, :]
    tail2 = ext2_ref[CONV_PAD + tm - (SC_CONV - 1):CONV_PAD + tm, :]
    sb_ref[...] = tail2
    ext2_ref[CONV_PAD - (SC_CONV - 1):CONV_PAD, :] = tail2
    y_b = _dot((sc[:, :D_MODEL] * u_sc).astype(BF16), wsco_ref[...])

    ext_ref[CONV_PAD:CONV_PAD + tm, :] = _dot(h, wqkv_ref[...])
    wc = wconv_ref[...]
    conv = 0.0
    for i in range(GDN_CONV):
        r0 = CONV_PAD - (GDN_CONV - 1) + i
        conv = conv + wc[i:i + 1, :] * ext_ref[r0:r0 + tm, :]
    tail = ext_ref[CONV_PAD + tm - (GDN_CONV - 1):CONV_PAD + tm, :]
    gb_ref[...] = tail
    ext_ref[CONV_PAD - (GDN_CONV - 1):CONV_PAD, :] = tail
    act = _silu(conv)
    nq = GDN_HEADS * GDN_DK
    for hd in range(GDN_HEADS):
        hs = slice(hd * GDN_DK, (hd + 1) * GDN_DK)
        qh = act[:, hd * GDN_DK:(hd + 1) * GDN_DK]
        kh = act[:, nq + hd * GDN_DK:nq + (hd + 1) * GDN_DK]
        q_s[:, hs] = qh * lax.rsqrt(jnp.sum(qh * qh, axis=-1, keepdims=True) + EPS) * (GDN_DK ** -0.5)
        k_s[:, hs] = kh * lax.rsqrt(jnp.sum(kh * kh, axis=-1, keepdims=True) + EPS)
    v_s[...] = act[:, 2 * nq:]

    ab = _dot(h, wab_ref[...])
    a = ab + dtb_ref[...]
    softplus = jnp.maximum(a, 0.0) + jnp.log1p(jnp.exp(-jnp.abs(a)))
    g = -jnp.exp(alog_ref[...]) * softplus
    ri = lax.broadcasted_iota(jnp.int32, (tm, tm), 0)
    ci = lax.broadcasted_iota(jnp.int32, (tm, tm), 1)
    same_chunk = (ri // CHUNK) == (ci // CHUNK)
    tri = jnp.where((ci <= ri) & same_chunk, 1.0, 0.0).astype(BF16)
    g_hi, g_mid, g_lo = _split3(g)
    gc = _dot(tri, g_hi) + _dot(tri, g_mid) + _dot(tri, g_lo)
    beta = jax.nn.sigmoid(ab)
    for hd in range(GDN_HEADS):
        hs = slice(hd * GDN_DK, (hd + 1) * GDN_DK)
        gc_s[:, hs] = jnp.broadcast_to(gc[:, hd:hd + 1], (tm, GDN_DK))
        be_s[:, hs] = jnp.broadcast_to(beta[:, GDN_HEADS + hd:GDN_HEADS + hd + 1], (tm, GDN_DK))

    z = _dot(h, wz_ref[...])
    gg = _dot(h, wg_ref[...])
    gate_a = jax.nn.sigmoid(gg[:, :D_MODEL])
    gate_b = jax.nn.sigmoid(gg[:, D_MODEL:])

    li = lax.broadcasted_iota(jnp.int32, (CHUNK, CHUNK), 0)
    mi = lax.broadcasted_iota(jnp.int32, (CHUNK, CHUNK), 1)
    lower_incl = li >= mi
    lower_strict = li > mi
    eye = jnp.where(li == mi, 1.0, 0.0)

    head_lanes = [slice(hd * GDN_DK, (hd + 1) * GDN_DK) for hd in range(GDN_HEADS)]
    pairs = [(slice(c * CHUNK, (c + 1) * CHUNK), hs) for c in range(nchunk) for hs in head_lanes]
    q = [q_s[r, hs] for r, hs in pairs]
    k = [k_s[r, hs] for r, hs in pairs]
    gcb = [gc_s[r, hs] for r, hs in pairs]
    bb = [be_s[r, hs] for r, hs in pairs]
    kb = [ki * bi for ki, bi in zip(k, bb)]
    kkqk = [_dot_nt(jnp.concatenate([kbi, qi], axis=0).astype(BF16), ki.astype(BF16))
            for kbi, qi, ki in zip(kb, q, k)]
    decay = [jnp.exp(jnp.where(lower_incl, g[:, :CHUNK] - g.T[:CHUNK, :], -jnp.inf)) for g in gcb]
    a_mat = [jnp.where(lower_strict, m[:CHUNK] * d, 0.0) for m, d in zip(kkqk, decay)]
    qk = [(m[CHUNK:] * d).astype(BF16) for m, d in zip(kkqk, decay)]
    tinv = [eye - a for a in a_mat]
    pw = [a.astype(BF16) for a in a_mat]
    for _ in range(5):
        pw = [_dot(p, p) for p in pw]
        pw = [p.astype(BF16) for p in pw]
        tinv = [ti + _dot(ti.astype(BF16), p) for ti, p in zip(tinv, pw)]
    e_gc = [jnp.exp(g) for g in gcb]
    rhs = [jnp.concatenate([v_s[r, hs] * bi, kbi * ei], axis=1).astype(BF16)
           for (r, hs), bi, kbi, ei in zip(pairs, bb, kb, e_gc)]
    sol = [_dot(ti.astype(BF16), ri) for ti, ri in zip(tinv, rhs)]
    g_end = [g[CHUNK - 1:CHUNK, :] for g in gcb]
    w_qe = [jnp.concatenate([so[:, GDN_DK:], qi * ei], axis=0).astype(BF16) for so, qi, ei in zip(sol, q, e_gc)]
    k_rem = [(ki * jnp.exp(ge - g)).astype(BF16) for ki, ge, g in zip(k, g_end, gcb)]
    e_end = [jnp.exp(ge) for ge in g_end]
    for c in range(nchunk):
        sel = range(c * GDN_HEADS, (c + 1) * GDN_HEADS)
        st = [s_ref[hd] for hd in range(GDN_HEADS)]
        ws_qs = [_dot(w_qe[i], st[hd].astype(BF16)) for hd, i in enumerate(sel)]
        u16 = [(sol[i][:, :GDN_DK] - ws_qs[hd][:CHUNK]).astype(BF16) for hd, i in enumerate(sel)]
        for hd, i in enumerate(sel):
            r, hs = pairs[i]
            o_s[r, hs] = ws_qs[hd][CHUNK:] + _dot(qk[i], u16[hd])
        for hd, i in enumerate(sel):
            s_ref[hd] = st[hd] * e_end[i] + _dot_tn(k_rem[i], u16[hd])

    gn = gnorm_ref[...]
    og = []
    for hd in range(GDN_HEADS):
        hs = slice(hd * GDN_DK, (hd + 1) * GDN_DK)
        og.append((_rms(o_s[:, hs], gn) * _silu(z[:, hs])).astype(BF16))
    y_a = _dot(jnp.concatenate(og, axis=-1), wgo_ref[...])

    y = gate_a * y_a + gate_b * y_b
    m = _dot(y.astype(BF16), wmo_ref[...])
    y_ref[...] = x + _rms(m, npost_ref[...])


def _const_spec(shape):
    zeros = (0,) * len(shape)
    return pl.BlockSpec(shape, lambda *_: zeros, pipeline_mode=pl.Buffered(1))


def _row_tile(n, pref):
    tm = min(n, pref)
    assert n % tm == 0, (n, tm)
    return tm


def _ffn_call(x2d, gpre, w_in, w_out, gpost):
    n = x2d.shape[0]
    tm = _row_tile(n, 256)
    row = pl.BlockSpec((tm, D_MODEL), lambda i: (i, 0))
    return pl.pallas_call(
        _ffn_kernel,
        grid=(n // tm,),
        in_specs=[row, _const_spec(gpre.shape), _const_spec(w_in.shape), _const_spec(w_out.shape),
                  _const_spec(gpost.shape)],
        out_specs=row,
        out_shape=jax.ShapeDtypeStruct(x2d.shape, F32),
        compiler_params=pltpu.CompilerParams(dimension_semantics=("arbitrary",),
                                             vmem_limit_bytes=VMEM_LIMIT_BYTES),
        name="ffn_half_step",
    )(x2d, gpre, w_in, w_out, gpost)


def _memkv_call(mem2d, g, w):
    n = mem2d.shape[0]
    tm = _row_tile(n, 256)
    row = pl.BlockSpec((tm, D_MODEL), lambda i: (i, 0))
    return pl.pallas_call(
        _memkv_kernel,
        grid=(n // tm,),
        in_specs=[row, _const_spec(g.shape), _const_spec(w.shape)],
        out_specs=[row, row],
        out_shape=[jax.ShapeDtypeStruct(mem2d.shape, F32)] * 2,
        compiler_params=pltpu.CompilerParams(dimension_semantics=("arbitrary",),
                                             vmem_limit_bytes=VMEM_LIMIT_BYTES),
        name="memory_kv",
    )(mem2d, g, w)


def _xattn_ffn_call(x, mk, mv, gxpre, wq, wo, gxpost, gpre, w_in, w_out, gpost):
    b, t, _ = x.shape
    tm = _row_tile(t, 256)
    row = pl.BlockSpec((None, tm, D_MODEL), lambda i, j: (i, j, 0))
    mem = pl.BlockSpec((None, N_MEM, D_MODEL), lambda i, j: (i, 0, 0))
    consts = (gxpre, wq, wo, gxpost, gpre, w_in, w_out, gpost)
    return pl.pallas_call(
        _xattn_ffn_kernel,
        grid=(b, t // tm),
        in_specs=[row, mem, mem] + [_const_spec(c.shape) for c in consts],
        out_specs=row,
        out_shape=jax.ShapeDtypeStruct(x.shape, F32),
        compiler_params=pltpu.CompilerParams(dimension_semantics=("arbitrary", "arbitrary"),
                                             vmem_limit_bytes=VMEM_LIMIT_BYTES),
        name="xattn_ffn",
    )(x, mk, mv, *consts)


def _mixer_call(x, s0, gb0, sb0, consts):
    b, t, _ = x.shape
    tm = _row_tile(t, 128)
    assert tm % CHUNK == 0, tm
    row = pl.BlockSpec((None, tm, D_MODEL), lambda i, j: (i, j, 0))
    s_spec = pl.BlockSpec((None, GDN_HEADS, GDN_DK, GDN_DK), lambda i, j: (i, 0, 0, 0))
    gb_spec = pl.BlockSpec((None, GDN_CONV - 1, GDN_QKV), lambda i, j: (i, 0, 0))
    sb_spec = pl.BlockSpec((None, SC_CONV - 1, D_MODEL), lambda i, j: (i, 0, 0))
    tile = lambda w: pltpu.VMEM((tm, w), F32)
    return pl.pallas_call(
        functools.partial(_mixer_kernel, tm=tm),
        grid=(b, t // tm),
        in_specs=[row, s_spec, gb_spec, sb_spec] + [_const_spec(c.shape) for c in consts],
        out_specs=[row, s_spec, gb_spec, sb_spec],
        out_shape=[jax.ShapeDtypeStruct(x.shape, F32), jax.ShapeDtypeStruct(s0.shape, F32),
                   jax.ShapeDtypeStruct(gb0.shape, F32), jax.ShapeDtypeStruct(sb0.shape, F32)],
        scratch_shapes=[pltpu.VMEM((CONV_PAD + tm, GDN_QKV), F32), pltpu.VMEM((CONV_PAD + tm, D_MODEL), F32),
                        tile(D_MODEL), tile(D_MODEL), tile(D_MODEL), tile(D_MODEL), tile(D_MODEL), tile(D_MODEL)],
        compiler_params=pltpu.CompilerParams(dimension_semantics=("arbitrary", "arbitrary"),
                                             vmem_limit_bytes=VMEM_LIMIT_BYTES),
        name="token_mixer",
    )(x, s0, gb0, sb0, *consts)


def _head_lanes_row(v):
    return jnp.pad(v.astype(F32), (0, GDN_DK - GDN_HEADS))[None, :]


def _layer(x, mk, mv, s0, gb0, sb0, p):
    b, t, _ = x.shape
    x = _ffn_call(x.reshape(b * t, D_MODEL), p['n_ffn1_pre'], p['w_ffn1_in'], p['w_ffn1_out'],
                  p['n_ffn1_post']).reshape(b, t, D_MODEL)
    x, s_new, gb_new, sb_new = _mixer_call(x, s0, gb0, sb0, p['mixer_consts'])
    x = _xattn_ffn_call(x, mk.reshape(b, N_MEM, D_MODEL), mv.reshape(b, N_MEM, D_MODEL),
                        p['n_x_pre'], p['w_xq'], p['w_xo'], p['n_x_post'],
                        p['n_ffn2_pre'], p['w_ffn2_in'], p['w_ffn2_out'], p['n_ffn2_post'])
    return x, s_new, gb_new, sb_new


def _prep_params(raw):
    p = {}
    for name, arr in raw.items():
        a = arr[0]
        if name.startswith('w_') and name not in ('w_gdn_conv', 'w_sc_conv', 'w_mix_in'):
            p[name] = a.astype(BF16)
        elif name.startswith('n_') or name == 'gdn_norm':
            p[name] = a[None, :]
    w = raw['w_mix_in'][0]
    nq = GDN_HEADS * GDN_DK
    c_z, c_a, c_b, c_sc = 3 * nq, 4 * nq, 4 * nq + GDN_HEADS, 4 * nq + 2 * GDN_HEADS
    c_g = c_sc + 3 * D_MODEL
    w_ab = jnp.pad(w[:, c_a:c_sc], ((0, 0), (0, GDN_DK - 2 * GDN_HEADS)))
    p['mixer_consts'] = (
        p['n_mix_pre'], w[:, :c_z].astype(BF16), w[:, c_z:c_a].astype(BF16), w_ab.astype(BF16),
        w[:, c_sc:c_g].astype(BF16), w[:, c_g:].astype(BF16),
        raw['w_gdn_conv'][0], _head_lanes_row(raw['gdn_a_log'][0]), _head_lanes_row(raw['gdn_dt_bias'][0]),
        p['gdn_norm'], p['w_gdn_out'], raw['w_sc_conv'][0], p['w_sc_out'], p['w_mix_out'], p['n_mix_post'])
    return p


def kernel(x_prompt, x_sample, mem_prompt, cache_mem_k, cache_mem_v, state_gdn, state_gdn_conv, state_shortconv, n_ffn1_pre, w_ffn1_in, w_ffn1_out, n_ffn1_post, n_mix_pre, w_mix_in, w_gdn_conv, gdn_a_log, gdn_dt_bias, gdn_norm, w_gdn_out, w_sc_conv, w_sc_out, w_mix_out, n_mix_post, n_x_pre, n_mem, w_xq, w_xkv, w_xo, n_x_post, n_ffn2_pre, w_ffn2_in, w_ffn2_out, n_ffn2_post):
    raw = dict(n_ffn1_pre=n_ffn1_pre, w_ffn1_in=w_ffn1_in, w_ffn1_out=w_ffn1_out, n_ffn1_post=n_ffn1_post,
               n_mix_pre=n_mix_pre, w_mix_in=w_mix_in, w_gdn_conv=w_gdn_conv, gdn_a_log=gdn_a_log,
               gdn_dt_bias=gdn_dt_bias, gdn_norm=gdn_norm, w_gdn_out=w_gdn_out, w_sc_conv=w_sc_conv,
               w_sc_out=w_sc_out, w_mix_out=w_mix_out, n_mix_post=n_mix_post, n_x_pre=n_x_pre,
               n_mem=n_mem, w_xq=w_xq, w_xkv=w_xkv, w_xo=w_xo, n_x_post=n_x_post,
               n_ffn2_pre=n_ffn2_pre, w_ffn2_in=w_ffn2_in, w_ffn2_out=w_ffn2_out, n_ffn2_post=n_ffn2_post)
    assert w_mix_in.shape[0] == 1, "single-layer model"
    p = _prep_params(raw)
    bp = x_prompt.shape[0]
    mk_p, mv_p = _memkv_call(mem_prompt.reshape(bp * N_MEM, D_MODEL), p['n_mem'], p['w_xkv'])
    mk_p = mk_p.reshape(bp, N_MEM, XA_HEADS, XA_DIM)
    mv_p = mv_p.reshape(bp, N_MEM, XA_HEADS, XA_DIM)
    s0 = jnp.zeros((bp, GDN_HEADS, GDN_DK, GDN_DK), F32)
    gb0 = jnp.zeros((bp, GDN_CONV - 1, GDN_QKV), F32)
    sb0 = jnp.zeros((bp, SC_CONV - 1, D_MODEL), F32)
    yp, sp, gbp, sbp = _layer(x_prompt, mk_p, mv_p, s0, gb0, sb0, p)
    ys, ss, gbs, sbs = _layer(x_sample, cache_mem_k[0], cache_mem_v[0], state_gdn[0], state_gdn_conv[0],
                              state_shortconv[0], p)
    return (yp, ys, sp[None], gbp[None], sbp[None], mk_p[None], mv_p[None], ss[None], gbs[None], sbs[None])
```

```python
import functools

import jax
import jax.numpy as jnp
from jax import lax
from jax.experimental import pallas as pl
from jax.experimental.pallas import tpu as pltpu

F32 = jnp.float32
BF16 = jnp.bfloat16

D_MODEL = 1024
D_FF = 2816
GDN_HEADS = 8
GDN_DK = 128
GDN_CONV = 4
GDN_QKV = 3 * GDN_HEADS * GDN_DK
SC_CONV = 3
N_MEM = 256
XA_HEADS = 4
XA_DIM = D_MODEL // XA_HEADS
CHUNK = 64
EPS = 1e-6
CONV_PAD = 8

VMEM_LIMIT_BYTES = 56 * 1024 * 1024


def _rms(x, g):
    return x * lax.rsqrt(jnp.mean(x * x, axis=-1, keepdims=True) + EPS) * g


def _dot(a, b):
    return jnp.dot(a, b, preferred_element_type=F32)


def _dot_nt(a, b):
    return lax.dot_general(a, b, (((1,), (1,)), ((), ())), preferred_element_type=F32)


def _dot_tn(a, b):
    return lax.dot_general(a, b, (((0,), (0,)), ((), ())), preferred_element_type=F32)


def _silu(x):
    return x * jax.nn.sigmoid(x)


def _swiglu_half_step(x, gpre, w_in, w_out, gpost):
    h = _rms(x, gpre).astype(BF16)
    gu = _dot(h, w_in)
    a = (_silu(gu[:, :D_FF]) * gu[:, D_FF:]).astype(BF16)
    y = _dot(a, w_out)
    return x + 0.5 * _rms(y, gpost)


def _ffn_kernel(x_ref, gpre_ref, win_ref, wout_ref, gpost_ref, o_ref):
    o_ref[...] = _swiglu_half_step(x_ref[...], gpre_ref[...], win_ref[...], wout_ref[...], gpost_ref[...])


def _memkv_kernel(m_ref, g_ref, w_ref, k_ref, v_ref):
    h = _rms(m_ref[...], g_ref[...]).astype(BF16)
    kv = _dot(h, w_ref[...])
    k_ref[...] = kv[:, :D_MODEL]
    v_ref[...] = kv[:, D_MODEL:]


def _xattn_ffn_kernel(x_ref, mk_ref, mv_ref, gxpre_ref, wq_ref, wo_ref, gxpost_ref,
                      gpre_ref, win_ref, wout_ref, gpost_ref, o_ref):
    x = x_ref[...]
    h = _rms(x, gxpre_ref[...]).astype(BF16)
    q = _dot(h, wq_ref[...]).astype(BF16)
    mk = mk_ref[...].astype(BF16)
    mv = mv_ref[...].astype(BF16)
    heads = []
    for hd in range(XA_HEADS):
        hs = slice(hd * XA_DIM, (hd + 1) * XA_DIM)
        s = _dot_nt(q[:, hs], mk[:, hs]) * (XA_DIM ** -0.5)
        e = jnp.exp(s - jnp.max(s, axis=-1, keepdims=True))
        p = (e * (1.0 / jnp.sum(e, axis=-1, keepdims=True))).astype(BF16)
        heads.append(_dot(p, mv[:, hs]))
    o = jnp.concatenate(heads, axis=-1).astype(BF16)
    x = x + _rms(_dot(o, wo_ref[...]), gxpost_ref[...])
    o_ref[...] = _swiglu_half_step(x, gpre_ref[...], win_ref[...], wout_ref[...], gpost_ref[...])


def _split3(x):
    hi = x.astype(BF16)
    r1 = x - hi.astype(F32)
    mid = r1.astype(BF16)
    lo = (r1 - mid.astype(F32)).astype(BF16)
    return hi, mid, lo


def _mixer_kernel(x_ref, s0_ref, gb0_ref, sb0_ref,
                  npre_ref, wqkv_ref, wz_ref, wab_ref, wsc_ref, wg_ref,
                  wconv_ref, alog_ref, dtb_ref, gnorm_ref, wgo_ref, wscc_ref, wsco_ref, wmo_ref, npost_ref,
                  y_ref, s_ref, gb_ref, sb_ref,
                  ext_ref, ext2_ref, q_s, k_s, v_s, gc_s, be_s, o_s, *, tm):
    t = pl.program_id(1)
    nchunk = tm // CHUNK

    @pl.when(t == 0)
    def _():
        s_ref[...] = s0_ref[...]
        ext_ref[CONV_PAD - (GDN_CONV - 1):CONV_PAD, :] = gb0_ref[...]
        ext2_ref[CONV_PAD - (SC_CONV - 1):CONV_PAD, :] = sb0_ref[...]

    x = x_ref[...]
    h = _rms(x, npre_ref[...]).astype(BF16)

    sc = _dot(h, wsc_ref[...])
    ext2_ref[CONV_PAD:CONV_PAD + tm, :] = sc[:, D_MODEL:2 * D_MODEL] * sc[:, 2 * D_MODEL:]
    w3 = wscc_ref[...]
    u_sc = 0.0
    for i in range(SC_CONV):
        r0 = CONV_PAD - (SC_CONV - 1) + i
        u_sc = u_sc + w3[i:i + 1, :] * ext2_ref[r0:r0 + tm, :]
    tail2 = ext2_ref[CONV_PAD + tm - (SC_CONV - 1):CONV_PAD + tm, :]
    sb_ref[...] = tail2
    ext2_ref[CONV_PAD - (SC_CONV - 1):CONV_PAD, :] = tail2
    y_b = _dot((sc[:, :D_MODEL] * u_sc).astype(BF16), wsco_ref[...])

    ext_ref[CONV_PAD:CONV_PAD + tm, :] = _dot(h, wqkv_ref[...])
    wc = wconv_ref[...]
    conv = 0.0
    for i in range(GDN_CONV):
        r0 = CONV_PAD - (GDN_CONV - 1) + i
        conv = conv + wc[i:i + 1, :] * ext_ref[r0:r0 + tm, :]
    tail = ext_ref[CONV_PAD + tm - (GDN_CONV - 1):CONV_PAD + tm, :]
    gb_ref[...] = tail
    ext_ref[CONV_PAD - (GDN_CONV - 1):CONV_PAD, :] = tail
    act = _silu(conv)
    nq = GDN_HEADS * GDN_DK
    for hd in range(GDN_HEADS):
        hs = slice(hd * GDN_DK, (hd + 1) * GDN_DK)
        qh = act[:, hd * GDN_DK:(hd + 1) * GDN_DK]
        kh = act[:, nq + hd * GDN_DK:nq + (hd + 1) * GDN_DK]
        q_s[:, hs] = qh * lax.rsqrt(jnp.sum(qh * qh, axis=-1, keepdims=True) + EPS) * (GDN_DK ** -0.5)
        k_s[:, hs] = kh * lax.rsqrt(jnp.sum(kh * kh, axis=-1, keepdims=True) + EPS)
    v_s[...] = act[:, 2 * nq:]

    ab = _dot(h, wab_ref[...])
    a = ab + dtb_ref[...]
    softplus = jnp.maximum(a, 0.0) + jnp.log1p(jnp.exp(-jnp.abs(a)))
    g = -jnp.exp(alog_ref[...]) * softplus
    ri = lax.broadcasted_iota(jnp.int32, (tm, tm), 0)
    ci = lax.broadcasted_iota(jnp.int32, (tm, tm), 1)
    same_chunk = (ri // CHUNK) == (ci // CHUNK)
    tri = jnp.where((ci <= ri) & same_chunk, 1.0, 0.0).astype(BF16)
    g_hi, g_mid, g_lo = _split3(g)
    gc = _dot(tri, g_hi) + _dot(tri, g_mid) + _dot(tri, g_lo)
    beta = jax.nn.sigmoid(ab)
    for hd in range(GDN_HEADS):
        hs = slice(hd * GDN_DK, (hd + 1) * GDN_DK)
        gc_s[:, hs] = jnp.broadcast_to(gc[:, hd:hd + 1], (tm, GDN_DK))
        be_s[:, hs] = jnp.broadcast_to(beta[:, GDN_HEADS + hd:GDN_HEADS + hd + 1], (tm, GDN_DK))

    z = _dot(h, wz_ref[...])
    gg = _dot(h, wg_ref[...])
    gate_a = jax.nn.sigmoid(gg[:, :D_MODEL])
    gate_b = jax.nn.sigmoid(gg[:, D_MODEL:])

    li = lax.broadcasted_iota(jnp.int32, (CHUNK, CHUNK), 0)
    mi = lax.broadcasted_iota(jnp.int32, (CHUNK, CHUNK), 1)
    lower_incl = li >= mi
    lower_strict = li > mi
    eye = jnp.where(li == mi, 1.0, 0.0)

    head_lanes = [slice(hd * GDN_DK, (hd + 1) * GDN_DK) for hd in range(GDN_HEADS)]
    pairs = [(slice(c * CHUNK, (c + 1) * CHUNK), hs) for c in range(nchunk) for hs in head_lanes]
    q = [q_s[r, hs] for r, hs in pairs]
    k = [k_s[r, hs] for r, hs in pairs]
    gcb = [gc_s[r, hs] for r, hs in pairs]
    bb = [be_s[r, hs] for r, hs in pairs]
    kb = [ki * bi for ki, bi in zip(k, bb)]
    kkqk = [_dot_nt(jnp.concatenate([kbi, qi], axis=0).astype(BF16), ki.astype(BF16))
            for kbi, qi, ki in zip(kb, q, k)]
    decay = [jnp.exp(jnp.where(lower_incl, g[:, :CHUNK] - g.T[:CHUNK, :], -jnp.inf)) for g in gcb]
    a_mat = [jnp.where(lower_strict, m[:CHUNK] * d, 0.0) for m, d in zip(kkqk, decay)]
    qk = [(m[CHUNK:] * d).astype(BF16) for m, d in zip(kkqk, decay)]
    tinv = [eye - a for a in a_mat]
    pw = [a.astype(BF16) for a in a_mat]
    for _ in range(5):
        pw = [_dot(p, p) for p in pw]
        pw = [p.astype(BF16) for p in pw]
        tinv = [ti + _dot(ti.astype(BF16), p) for ti, p in zip(tinv, pw)]
    e_gc = [jnp.exp(g) for g in gcb]
    rhs = [jnp.concatenate([v_s[r, hs] * bi, kbi * ei], axis=1).astype(BF16)
           for (r, hs), bi, kbi, ei in zip(pairs, bb, kb, e_gc)]
    sol = [_dot(ti.astype(BF16), ri) for ti, ri in zip(tinv, rhs)]
    g_end = [g[CHUNK - 1:CHUNK, :] for g in gcb]
    w_qe = [jnp.concatenate([so[:, GDN_DK:], qi * ei], axis=0).astype(BF16) for so, qi, ei in zip(sol, q, e_gc)]
    k_rem = [(ki * jnp.exp(ge - g)).astype(BF16) for ki, ge, g in zip(k, g_end, gcb)]
    e_end = [jnp.exp(ge) for ge in g_end]
    for c in range(nchunk):
        sel = range(c * GDN_HEADS, (c + 1) * GDN_HEADS)
        st = [s_ref[hd] for hd in range(GDN_HEADS)]
        ws_qs = [_dot(w_qe[i], st[hd].astype(BF16)) for hd, i in enumerate(sel)]
        u16 = [(sol[i][:, :GDN_DK] - ws_qs[hd][:CHUNK]).astype(BF16) for hd, i in enumerate(sel)]
        for hd, i in enumerate(sel):
            r, hs = pairs[i]
            o_s[r, hs] = ws_qs[hd][CHUNK:] + _dot(qk[i], u16[hd])
        for hd, i in enumerate(sel):
            s_ref[hd] = st[hd] * e_end[i] + _dot_tn(k_rem[i], u16[hd])

    gn = gnorm_ref[...]
    og = []
    for hd in range(GDN_HEADS):
        hs = slice(hd * GDN_DK, (hd + 1) * GDN_DK)
        og.append((_rms(o_s[:, hs], gn) * _silu(z[:, hs])).astype(BF16))
    y_a = _dot(jnp.concatenate(og, axis=-1), wgo_ref[...])

    y = gate_a * y_a + gate_b * y_b
    m = _dot(y.astype(BF16), wmo_ref[...])
    y_ref[...] = x + _rms(m, npost_ref[...])


def _const_spec(shape):
    zeros = (0,) * len(shape)
    return pl.BlockSpec(shape, lambda *_: zeros, pipeline_mode=pl.Buffered(1))


def _row_tile(n, pref):
    tm = min(n, pref)
    assert n % tm == 0, (n, tm)
    return tm


def _ffn_call(x2d, gpre, w_in, w_out, gpost):
    n = x2d.shape[0]
    tm = _row_tile(n, 512)
    row = pl.BlockSpec((tm, D_MODEL), lambda i: (i, 0))
    return pl.pallas_call(
        _ffn_kernel,
        grid=(n // tm,),
        in_specs=[row, _const_spec(gpre.shape), _const_spec(w_in.shape), _const_spec(w_out.shape),
                  _const_spec(gpost.shape)],
        out_specs=row,
        out_shape=jax.ShapeDtypeStruct(x2d.shape, F32),
        compiler_params=pltpu.CompilerParams(dimension_semantics=("arbitrary",),
                                             vmem_limit_bytes=VMEM_LIMIT_BYTES),
        name="ffn_half_step",
    )(x2d, gpre, w_in, w_out, gpost)


def _memkv_call(mem2d, g, w):
    n = mem2d.shape[0]
    tm = _row_tile(n, 256)
    row = pl.BlockSpec((tm, D_MODEL), lambda i: (i, 0))
    return pl.pallas_call(
        _memkv_kernel,
        grid=(n // tm,),
        in_specs=[row, _const_spec(g.shape), _const_spec(w.shape)],
        out_specs=[row, row],
        out_shape=[jax.ShapeDtypeStruct(mem2d.shape, F32)] * 2,
        compiler_params=pltpu.CompilerParams(dimension_semantics=("arbitrary",),
                                             vmem_limit_bytes=VMEM_LIMIT_BYTES),
        name="memory_kv",
    )(mem2d, g, w)


def _xattn_ffn_call(x, mk, mv, gxpre, wq, wo, gxpost, gpre, w_in, w_out, gpost):
    b, t, _ = x.shape
    tm = _row_tile(t, 512)
    row = pl.BlockSpec((None, tm, D_MODEL), lambda i, j: (i, j, 0))
    mem = pl.BlockSpec((None, N_MEM, D_MODEL), lambda i, j: (i, 0, 0))
    consts = (gxpre, wq, wo, gxpost, gpre, w_in, w_out, gpost)
    return pl.pallas_call(
        _xattn_ffn_kernel,
        grid=(b, t // tm),
        in_specs=[row, mem, mem] + [_const_spec(c.shape) for c in consts],
        out_specs=row,
        out_shape=jax.ShapeDtypeStruct(x.shape, F32),
        compiler_params=pltpu.CompilerParams(dimension_semantics=("arbitrary", "arbitrary"),
                                             vmem_limit_bytes=VMEM_LIMIT_BYTES),
        name="xattn_ffn",
    )(x, mk, mv, *consts)


def _mixer_call(x, s0, gb0, sb0, consts):
    b, t, _ = x.shape
    tm = _row_tile(t, 256)
    assert tm % CHUNK == 0, tm
    row = pl.BlockSpec((None, tm, D_MODEL), lambda i, j: (i, j, 0))
    s_spec = pl.BlockSpec((None, GDN_HEADS, GDN_DK, GDN_DK), lambda i, j: (i, 0, 0, 0))
    gb_spec = pl.BlockSpec((None, GDN_CONV - 1, GDN_QKV), lambda i, j: (i, 0, 0))
    sb_spec = pl.BlockSpec((None, SC_CONV - 1, D_MODEL), lambda i, j: (i, 0, 0))
    tile = lambda w: pltpu.VMEM((tm, w), F32)
    return pl.pallas_call(
        functools.partial(_mixer_kernel, tm=tm),
        grid=(b, t // tm),
        in_specs=[row, s_spec, gb_spec, sb_spec] + [_const_spec(c.shape) for c in consts],
        out_specs=[row, s_spec, gb_spec, sb_spec],
        out_shape=[jax.ShapeDtypeStruct(x.shape, F32), jax.ShapeDtypeStruct(s0.shape, F32),
                   jax.ShapeDtypeStruct(gb0.shape, F32), jax.ShapeDtypeStruct(sb0.shape, F32)],
        scratch_shapes=[pltpu.VMEM((CONV_PAD + tm, GDN_QKV), F32), pltpu.VMEM((CONV_PAD + tm, D_MODEL), F32),
                        tile(D_MODEL), tile(D_MODEL), tile(D_MODEL), tile(D_MODEL), tile(D_MODEL), tile(D_MODEL)],
        compiler_params=pltpu.CompilerParams(dimension_semantics=("arbitrary", "arbitrary"),
                                             vmem_limit_bytes=VMEM_LIMIT_BYTES),
        name="token_mixer",
    )(x, s0, gb0, sb0, *consts)


def _head_lanes_row(v):
    return jnp.pad(v.astype(F32), (0, GDN_DK - GDN_HEADS))[None, :]


def _layer(x, mk, mv, s0, gb0, sb0, p):
    b, t, _ = x.shape
    x = _ffn_call(x.reshape(b * t, D_MODEL), p['n_ffn1_pre'], p['w_ffn1_in'], p['w_ffn1_out'],
                  p['n_ffn1_post']).reshape(b, t, D_MODEL)
    x, s_new, gb_new, sb_new = _mixer_call(x, s0, gb0, sb0, p['mixer_consts'])
    x = _xattn_ffn_call(x, mk.reshape(b, N_MEM, D_MODEL), mv.reshape(b, N_MEM, D_MODEL),
                        p['n_x_pre'], p['w_xq'], p['w_xo'], p['n_x_post'],
                        p['n_ffn2_pre'], p['w_ffn2_in'], p['w_ffn2_out'], p['n_ffn2_post'])
    return x, s_new, gb_new, sb_new


def _prep_params(raw):
    p = {}
    for name, arr in raw.items():
        a = arr[0]
        if name.startswith('w_') and name not in ('w_gdn_conv', 'w_sc_conv', 'w_mix_in'):
            p[name] = a.astype(BF16)
        elif name.startswith('n_') or name == 'gdn_norm':
            p[name] = a[None, :]
    w = raw['w_mix_in'][0]
    nq = GDN_HEADS * GDN_DK
    c_z, c_a, c_b, c_sc = 3 * nq, 4 * nq, 4 * nq + GDN_HEADS, 4 * nq + 2 * GDN_HEADS
    c_g = c_sc + 3 * D_MODEL
    w_ab = jnp.pad(w[:, c_a:c_sc], ((0, 0), (0, GDN_DK - 2 * GDN_HEADS)))
    p['mixer_consts'] = (
        p['n_mix_pre'], w[:, :c_z].astype(BF16), w[:, c_z:c_a].astype(BF16), w_ab.astype(BF16),
        w[:, c_sc:c_g].astype(BF16), w[:, c_g:].astype(BF16),
        raw['w_gdn_conv'][0], _head_lanes_row(raw['gdn_a_log'][0]), _head_lanes_row(raw['gdn_dt_bias'][0]),
        p['gdn_norm'], p['w_gdn_out'], raw['w_sc_conv'][0], p['w_sc_out'], p['w_mix_out'], p['n_mix_post'])
    return p


def kernel(x_prompt, x_sample, mem_prompt, cache_mem_k, cache_mem_v, state_gdn, state_gdn_conv, state_shortconv, n_ffn1_pre, w_ffn1_in, w_ffn1_out, n_ffn1_post, n_mix_pre, w_mix_in, w_gdn_conv, gdn_a_log, gdn_dt_bias, gdn_norm, w_gdn_out, w_sc_conv, w_sc_out, w_mix_out, n_mix_post, n_x_pre, n_mem, w_xq, w_xkv, w_xo, n_x_post, n_ffn2_pre, w_ffn2_in, w_ffn2_out, n_ffn2_post):
    raw = dict(n_ffn1_pre=n_ffn1_pre, w_ffn1_in=w_ffn1_in, w_ffn1_out=w_ffn1_out, n_ffn1_post=n_ffn1_post,
               n_mix_pre=n_mix_pre, w_mix_in=w_mix_in, w_gdn_conv=w_gdn_conv, gdn_a_log=gdn_a_log,
               gdn_dt_bias=gdn_dt_bias, gdn_norm=gdn_norm, w_gdn_out=w_gdn_out, w_sc_conv=w_sc_conv,
               w_sc_out=w_sc_out, w_mix_out=w_mix_out, n_mix_post=n_mix_post, n_x_pre=n_x_pre,
               n_mem=n_mem, w_xq=w_xq, w_xkv=w_xkv, w_xo=w_xo, n_x_post=n_x_post,
               n_ffn2_pre=n_ffn2_pre, w_ffn2_in=w_ffn2_in, w_ffn2_out=w_ffn2_out, n_ffn2_post=n_ffn2_post)
    assert w_mix_in.shape[0] == 1, "single-layer model"
    p = _prep_params(raw)
    bp = x_prompt.shape[0]
    mk_p, mv_p = _memkv_call(mem_prompt.reshape(bp * N_MEM, D_MODEL), p['n_mem'], p['w_xkv'])
    mk_p = mk_p.reshape(bp, N_MEM, XA_HEADS, XA_DIM)
    mv_p = mv_p.reshape(bp, N_MEM, XA_HEADS, XA_DIM)
    s0 = jnp.zeros((bp, GDN_HEADS, GDN_DK, GDN_DK), F32)
    gb0 = jnp.zeros((bp, GDN_CONV - 1, GDN_QKV), F32)
    sb0 = jnp.zeros((bp, SC_CONV - 1, D_MODEL), F32)
    yp, sp, gbp, sbp = _layer(x_prompt, mk_p, mv_p, s0, gb0, sb0, p)
    ys, ss, gbs, sbs = _layer(x_sample, cache_mem_k[0], cache_mem_v[0], state_gdn[0], state_gdn_conv[0],
                              state_shortconv[0], p)
    return (yp, ys, sp[None], gbp[None], sbp[None], mk_p[None], mv_p[None], ss[None], gbs[None], sbs[None])
```

```python
import functools

import jax
import jax.numpy as jnp
from jax import lax
from jax.experimental import pallas as pl
from jax.experimental.pallas import tpu as pltpu

F32 = jnp.float32
BF16 = jnp.bfloat16

D_MODEL = 1024
D_FF = 2816
GDN_HEADS = 8
GDN_DK = 128
GDN_CONV = 4
GDN_QKV = 3 * GDN_HEADS * GDN_DK
SC_CONV = 3
N_MEM = 256
XA_HEADS = 4
XA_DIM = D_MODEL // XA_HEADS
CHUNK = 64
EPS = 1e-6
CONV_PAD = 8

VMEM_LIMIT_BYTES = 56 * 1024 * 1024


def _rms(x, g):
    return x * lax.rsqrt(jnp.mean(x * x, axis=-1, keepdims=True) + EPS) * g


def _dot(a, b):
    return jnp.dot(a, b, preferred_element_type=F32)


def _dot_nt(a, b):
    return lax.dot_general(a, b, (((1,), (1,)), ((), ())), preferred_element_type=F32)


def _dot_tn(a, b):
    return lax.dot_general(a, b, (((0,), (0,)), ((), ())), preferred_element_type=F32)


def _silu(x):
    return x * jax.nn.sigmoid(x)


def _swiglu_half_step(x, gpre, w_in, w_out, gpost):
    h = _rms(x, gpre).astype(BF16)
    gu = _dot(h, w_in)
    a = (_silu(gu[:, :D_FF]) * gu[:, D_FF:]).astype(BF16)
    y = _dot(a, w_out)
    return x + 0.5 * _rms(y, gpost)


def _ffn_kernel(x_ref, gpre_ref, win_ref, wout_ref, gpost_ref, o_ref):
    o_ref[...] = _swiglu_half_step(x_ref[...], gpre_ref[...], win_ref[...], wout_ref[...], gpost_ref[...])


def _memkv_kernel(m_ref, g_ref, w_ref, k_ref, v_ref):
    h = _rms(m_ref[...], g_ref[...]).astype(BF16)
    kv = _dot(h, w_ref[...])
    for hd in range(XA_HEADS):
        k_ref[:, hd, :] = kv[:, hd * XA_DIM:(hd + 1) * XA_DIM]
        v_ref[:, hd, :] = kv[:, D_MODEL + hd * XA_DIM:D_MODEL + (hd + 1) * XA_DIM]


def _xattn_ffn_kernel(x_ref, mk_ref, mv_ref, gxpre_ref, wq_ref, wo_ref, gxpost_ref,
                      gpre_ref, win_ref, wout_ref, gpost_ref, o_ref, k16_ref, v16_ref):
    @pl.when(pl.program_id(1) == 0)
    def _():
        for hd in range(XA_HEADS):
            hs = slice(hd * XA_DIM, (hd + 1) * XA_DIM)
            k16_ref[:, hs] = mk_ref[:, hd, :].astype(BF16)
            v16_ref[:, hs] = mv_ref[:, hd, :].astype(BF16)

    x = x_ref[...]
    h = _rms(x, gxpre_ref[...]).astype(BF16)
    q = _dot(h, wq_ref[...]).astype(BF16)
    heads = []
    for hd in range(XA_HEADS):
        hs = slice(hd * XA_DIM, (hd + 1) * XA_DIM)
        s = _dot_nt(q[:, hs], k16_ref[:, hs]) * (XA_DIM ** -0.5)
        e = jnp.exp(s - jnp.max(s, axis=-1, keepdims=True))
        p = (e * (1.0 / jnp.sum(e, axis=-1, keepdims=True))).astype(BF16)
        heads.append(_dot(p, v16_ref[:, hs]))
    o = jnp.concatenate(heads, axis=-1).astype(BF16)
    x = x + _rms(_dot(o, wo_ref[...]), gxpost_ref[...])
    o_ref[...] = _swiglu_half_step(x, gpre_ref[...], win_ref[...], wout_ref[...], gpost_ref[...])


def _split3(x):
    hi = x.astype(BF16)
    r1 = x - hi.astype(F32)
    mid = r1.astype(BF16)
    lo = (r1 - mid.astype(F32)).astype(BF16)
    return hi, mid, lo


def _mixer_kernel(x_ref, s0_ref, gb0_ref, sb0_ref,
                  npre_ref, wqkv_ref, wz_ref, wab_ref, wsc_ref, wg_ref,
                  wconv_ref, alog_ref, dtb_ref, gnorm_ref, wgo_ref, wscc_ref, wsco_ref, wmo_ref, npost_ref,
                  y_ref, s_ref, gb_ref, sb_ref,
                  ext_ref, ext2_ref, q_s, k_s, v_s, gc_s, be_s, o_s, *, tm):
    t = pl.program_id(1)
    nchunk = tm // CHUNK

    @pl.when(t == 0)
    def _():
        s_ref[...] = s0_ref[...]
        ext_ref[CONV_PAD - (GDN_CONV - 1):CONV_PAD, :] = gb0_ref[...]
        ext2_ref[CONV_PAD - (SC_CONV - 1):CONV_PAD, :] = sb0_ref[...]

    x = x_ref[...]
    h = _rms(x, npre_ref[...]).astype(BF16)

    sc = _dot(h, wsc_ref[...])
    ext2_ref[CONV_PAD:CONV_PAD + tm, :] = sc[:, D_MODEL:2 * D_MODEL] * sc[:, 2 * D_MODEL:]
    w3 = wscc_ref[...]
    u_sc = 0.0
    for i in range(SC_CONV):
        r0 = CONV_PAD - (SC_CONV - 1) + i
        u_sc = u_sc + w3[i:i + 1, :] * ext2_ref[r0:r0 + tm, :]
    tail2 = ext2_ref[CONV_PAD + tm - (SC_CONV - 1):CONV_PAD + tm, :]
    sb_ref[...] = tail2
    ext2_ref[CONV_PAD - (SC_CONV - 1):CONV_PAD, :] = tail2
    y_b = _dot((sc[:, :D_MODEL] * u_sc).astype(BF16), wsco_ref[...])

    ext_ref[CONV_PAD:CONV_PAD + tm, :] = _dot(h, wqkv_ref[...])
    wc = wconv_ref[...]
    conv = 0.0
    for i in range(GDN_CONV):
        r0 = CONV_PAD - (GDN_CONV - 1) + i
        conv = conv + wc[i:i + 1, :] * ext_ref[r0:r0 + tm, :]
    tail = ext_ref[CONV_PAD + tm - (GDN_CONV - 1):CONV_PAD + tm, :]
    gb_ref[...] = tail
    ext_ref[CONV_PAD - (GDN_CONV - 1):CONV_PAD, :] = tail
    act = _silu(conv)
    nq = GDN_HEADS * GDN_DK
    for hd in range(GDN_HEADS):
        hs = slice(hd * GDN_DK, (hd + 1) * GDN_DK)
        qh = act[:, hd * GDN_DK:(hd + 1) * GDN_DK]
        kh = act[:, nq + hd * GDN_DK:nq + (hd + 1) * GDN_DK]
        q_s[:, hs] = qh * lax.rsqrt(jnp.sum(qh * qh, axis=-1, keepdims=True) + EPS) * (GDN_DK ** -0.5)
        k_s[:, hs] = kh * lax.rsqrt(jnp.sum(kh * kh, axis=-1, keepdims=True) + EPS)
    v_s[...] = act[:, 2 * nq:]

    ab = _dot(h, wab_ref[...])
    a = ab + dtb_ref[...]
    softplus = jnp.maximum(a, 0.0) + jnp.log1p(jnp.exp(-jnp.abs(a)))
    g = -jnp.exp(alog_ref[...]) * softplus
    ri = lax.broadcasted_iota(jnp.int32, (tm, tm), 0)
    ci = lax.broadcasted_iota(jnp.int32, (tm, tm), 1)
    same_chunk = (ri // CHUNK) == (ci // CHUNK)
    tri = jnp.where((ci <= ri) & same_chunk, 1.0, 0.0).astype(BF16)
    g_hi, g_mid, g_lo = _split3(g)
    gc = _dot(tri, g_hi) + _dot(tri, g_mid) + _dot(tri, g_lo)
    beta = jax.nn.sigmoid(ab)
    for hd in range(GDN_HEADS):
        hs = slice(hd * GDN_DK, (hd + 1) * GDN_DK)
        gc_s[:, hs] = jnp.broadcast_to(gc[:, hd:hd + 1], (tm, GDN_DK))
        be_s[:, hs] = jnp.broadcast_to(beta[:, GDN_HEADS + hd:GDN_HEADS + hd + 1], (tm, GDN_DK))

    z = _dot(h, wz_ref[...])
    gg = _dot(h, wg_ref[...])
    gate_a = jax.nn.sigmoid(gg[:, :D_MODEL])
    gate_b = jax.nn.sigmoid(gg[:, D_MODEL:])

    li = lax.broadcasted_iota(jnp.int32, (CHUNK, CHUNK), 0)
    mi = lax.broadcasted_iota(jnp.int32, (CHUNK, CHUNK), 1)
    lower_incl = li >= mi
    lower_strict = li > mi
    eye = jnp.where(li == mi, 1.0, 0.0)

    head_lanes = [slice(hd * GDN_DK, (hd + 1) * GDN_DK) for hd in range(GDN_HEADS)]
    pairs = [(slice(c * CHUNK, (c + 1) * CHUNK), hs) for c in range(nchunk) for hs in head_lanes]
    q = [q_s[r, hs] for r, hs in pairs]
    k = [k_s[r, hs] for r, hs in pairs]
    gcb = [gc_s[r, hs] for r, hs in pairs]
    bb = [be_s[r, hs] for r, hs in pairs]
    kb = [ki * bi for ki, bi in zip(k, bb)]
    kkqk = [_dot_nt(jnp.concatenate([kbi, qi], axis=0).astype(BF16), ki.astype(BF16))
            for kbi, qi, ki in zip(kb, q, k)]
    decay = [jnp.exp(jnp.where(lower_incl, g[:, :CHUNK] - g.T[:CHUNK, :], -jnp.inf)) for g in gcb]
    a_mat = [jnp.where(lower_strict, m[:CHUNK] * d, 0.0) for m, d in zip(kkqk, decay)]
    qk = [(m[CHUNK:] * d).astype(BF16) for m, d in zip(kkqk, decay)]
    tinv = [eye - a for a in a_mat]
    pw = [a.astype(BF16) for a in a_mat]
    for _ in range(5):
        pw = [_dot(p, p) for p in pw]
        pw = [p.astype(BF16) for p in pw]
        tinv = [ti + _dot(ti.astype(BF16), p) for ti, p in zip(tinv, pw)]
    e_gc = [jnp.exp(g) for g in gcb]
    rhs = [jnp.concatenate([v_s[r, hs] * bi, kbi * ei], axis=1).astype(BF16)
           for (r, hs), bi, kbi, ei in zip(pairs, bb, kb, e_gc)]
    sol = [_dot(ti.astype(BF16), ri) for ti, ri in zip(tinv, rhs)]
    g_end = [g[CHUNK - 1:CHUNK, :] for g in gcb]
    w_qe = [jnp.concatenate([so[:, GDN_DK:], qi * ei], axis=0).astype(BF16) for so, qi, ei in zip(sol, q, e_gc)]
    k_rem = [(ki * jnp.exp(ge - g)).astype(BF16) for ki, ge, g in zip(k, g_end, gcb)]
    e_end = [jnp.exp(ge) for ge in g_end]
    for c in range(nchunk):
        sel = range(c * GDN_HEADS, (c + 1) * GDN_HEADS)
        st = [s_ref[hd] for hd in range(GDN_HEADS)]
        ws_qs = [_dot(w_qe[i], st[hd].astype(BF16)) for hd, i in enumerate(sel)]
        u16 = [(sol[i][:, :GDN_DK] - ws_qs[hd][:CHUNK]).astype(BF16) for hd, i in enumerate(sel)]
        for hd, i in enumerate(sel):
            r, hs = pairs[i]
            o_s[r, hs] = ws_qs[hd][CHUNK:] + _dot(qk[i], u16[hd])
        for hd, i in enumerate(sel):
            s_ref[hd] = st[hd] * e_end[i] + _dot_tn(k_rem[i], u16[hd])

    gn = gnorm_ref[...]
    og = []
    for hd in range(GDN_HEADS):
        hs = slice(hd * GDN_DK, (hd + 1) * GDN_DK)
        og.append((_rms(o_s[:, hs], gn) * _silu(z[:, hs])).astype(BF16))
    y_a = _dot(jnp.concatenate(og, axis=-1), wgo_ref[...])

    y = gate_a * y_a + gate_b * y_b
    m = _dot(y.astype(BF16), wmo_ref[...])
    y_ref[...] = x + _rms(m, npost_ref[...])


def _const_spec(shape):
    zeros = (0,) * len(shape)
    return pl.BlockSpec(shape, lambda *_: zeros, pipeline_mode=pl.Buffered(1))


def _row_tile(n, pref):
    tm = min(n, pref)
    assert n % tm == 0, (n, tm)
    return tm


def _ffn_call(x2d, gpre, w_in, w_out, gpost):
    n = x2d.shape[0]
    tm = _row_tile(n, 512)
    row = pl.BlockSpec((tm, D_MODEL), lambda i: (i, 0))
    return pl.pallas_call(
        _ffn_kernel,
        grid=(n // tm,),
        in_specs=[row, _const_spec(gpre.shape), _const_spec(w_in.shape), _const_spec(w_out.shape),
                  _const_spec(gpost.shape)],
        out_specs=row,
        out_shape=jax.ShapeDtypeStruct(x2d.shape, F32),
        compiler_params=pltpu.CompilerParams(dimension_semantics=("arbitrary",),
                                             vmem_limit_bytes=VMEM_LIMIT_BYTES),
        name="ffn_half_step",
    )(x2d, gpre, w_in, w_out, gpost)


def _memkv_call(mem2d, g, w):
    n = mem2d.shape[0]
    tm = _row_tile(n, 256)
    row = pl.BlockSpec((tm, D_MODEL), lambda i: (i, 0))
    kv_spec = pl.BlockSpec((tm, XA_HEADS, XA_DIM), lambda i: (i, 0, 0))
    return pl.pallas_call(
        _memkv_kernel,
        grid=(n // tm,),
        in_specs=[row, _const_spec(g.shape), _const_spec(w.shape)],
        out_specs=[kv_spec, kv_spec],
        out_shape=[jax.ShapeDtypeStruct((n, XA_HEADS, XA_DIM), F32)] * 2,
        compiler_params=pltpu.CompilerParams(dimension_semantics=("arbitrary",),
                                             vmem_limit_bytes=VMEM_LIMIT_BYTES),
        name="memory_kv",
    )(mem2d, g, w)


def _xattn_ffn_call(x, mk, mv, gxpre, wq, wo, gxpost, gpre, w_in, w_out, gpost):
    b, t, _ = x.shape
    tm = _row_tile(t, 512)
    row = pl.BlockSpec((None, tm, D_MODEL), lambda i, j: (i, j, 0))
    mem = pl.BlockSpec((None, N_MEM, XA_HEADS, XA_DIM), lambda i, j: (i, 0, 0, 0))
    consts = (gxpre, wq, wo, gxpost, gpre, w_in, w_out, gpost)
    return pl.pallas_call(
        _xattn_ffn_kernel,
        grid=(b, t // tm),
        in_specs=[row, mem, mem] + [_const_spec(c.shape) for c in consts],
        out_specs=row,
        out_shape=jax.ShapeDtypeStruct(x.shape, F32),
        scratch_shapes=[pltpu.VMEM((N_MEM, D_MODEL), BF16), pltpu.VMEM((N_MEM, D_MODEL), BF16)],
        compiler_params=pltpu.CompilerParams(dimension_semantics=("arbitrary", "arbitrary"),
                                             vmem_limit_bytes=VMEM_LIMIT_BYTES),
        name="xattn_ffn",
    )(x, mk, mv, *consts)


def _mixer_call(x, s0, gb0, sb0, consts):
    b, t, _ = x.shape
    tm = _row_tile(t, 256)
    assert tm % CHUNK == 0, tm
    row = pl.BlockSpec((None, tm, D_MODEL), lambda i, j: (i, j, 0))
    s_spec = pl.BlockSpec((None, GDN_HEADS, GDN_DK, GDN_DK), lambda i, j: (i, 0, 0, 0))
    gb_spec = pl.BlockSpec((None, GDN_CONV - 1, GDN_QKV), lambda i, j: (i, 0, 0))
    sb_spec = pl.BlockSpec((None, SC_CONV - 1, D_MODEL), lambda i, j: (i, 0, 0))
    tile = lambda w: pltpu.VMEM((tm, w), F32)
    return pl.pallas_call(
        functools.partial(_mixer_kernel, tm=tm),
        grid=(b, t // tm),
        in_specs=[row, s_spec, gb_spec, sb_spec] + [_const_spec(c.shape) for c in consts],
        out_specs=[row, s_spec, gb_spec, sb_spec],
        out_shape=[jax.ShapeDtypeStruct(x.shape, F32), jax.ShapeDtypeStruct(s0.shape, F32),
                   jax.ShapeDtypeStruct(gb0.shape, F32), jax.ShapeDtypeStruct(sb0.shape, F32)],
        scratch_shapes=[pltpu.VMEM((CONV_PAD + tm, GDN_QKV), F32), pltpu.VMEM((CONV_PAD + tm, D_MODEL), F32),
                        tile(D_MODEL), tile(D_MODEL), tile(D_MODEL), tile(D_MODEL), tile(D_MODEL), tile(D_MODEL)],
        compiler_params=pltpu.CompilerParams(dimension_semantics=("arbitrary", "arbitrary"),
                                             vmem_limit_bytes=VMEM_LIMIT_BYTES),
        name="token_mixer",
    )(x, s0, gb0, sb0, *consts)


def _head_lanes_row(v):
    return jnp.pad(v.astype(F32), (0, GDN_DK - GDN_HEADS))[None, :]


def _layer(x, mk, mv, s0, gb0, sb0, p):
    b, t, _ = x.shape
    x = _ffn_call(x.reshape(b * t, D_MODEL), p['n_ffn1_pre'], p['w_ffn1_in'], p['w_ffn1_out'],
                  p['n_ffn1_post']).reshape(b, t, D_MODEL)
    x, s_new, gb_new, sb_new = _mixer_call(x, s0, gb0, sb0, p['mixer_consts'])
    x = _xattn_ffn_call(x, mk, mv,
                        p['n_x_pre'], p['w_xq'], p['w_xo'], p['n_x_post'],
                        p['n_ffn2_pre'], p['w_ffn2_in'], p['w_ffn2_out'], p['n_ffn2_post'])
    return x, s_new, gb_new, sb_new


def _prep_params(raw):
    p = {}
    for name, arr in raw.items():
        a = arr[0]
        if name.startswith('w_') and name not in ('w_gdn_conv', 'w_sc_conv', 'w_mix_in'):
            p[name] = a.astype(BF16)
        elif name.startswith('n_') or name == 'gdn_norm':
            p[name] = a[None, :]
    w = raw['w_mix_in'][0]
    nq = GDN_HEADS * GDN_DK
    c_z, c_a, c_b, c_sc = 3 * nq, 4 * nq, 4 * nq + GDN_HEADS, 4 * nq + 2 * GDN_HEADS
    c_g = c_sc + 3 * D_MODEL
    w_ab = jnp.pad(w[:, c_a:c_sc], ((0, 0), (0, GDN_DK - 2 * GDN_HEADS)))
    p['mixer_consts'] = (
        p['n_mix_pre'], w[:, :c_z].astype(BF16), w[:, c_z:c_a].astype(BF16), w_ab.astype(BF16),
        w[:, c_sc:c_g].astype(BF16), w[:, c_g:].astype(BF16),
        raw['w_gdn_conv'][0], _head_lanes_row(raw['gdn_a_log'][0]), _head_lanes_row(raw['gdn_dt_bias'][0]),
        p['gdn_norm'], p['w_gdn_out'], raw['w_sc_conv'][0], p['w_sc_out'], p['w_mix_out'], p['n_mix_post'])
    return p


def kernel(x_prompt, x_sample, mem_prompt, cache_mem_k, cache_mem_v, state_gdn, state_gdn_conv, state_shortconv, n_ffn1_pre, w_ffn1_in, w_ffn1_out, n_ffn1_post, n_mix_pre, w_mix_in, w_gdn_conv, gdn_a_log, gdn_dt_bias, gdn_norm, w_gdn_out, w_sc_conv, w_sc_out, w_mix_out, n_mix_post, n_x_pre, n_mem, w_xq, w_xkv, w_xo, n_x_post, n_ffn2_pre, w_ffn2_in, w_ffn2_out, n_ffn2_post):
    raw = dict(n_ffn1_pre=n_ffn1_pre, w_ffn1_in=w_ffn1_in, w_ffn1_out=w_ffn1_out, n_ffn1_post=n_ffn1_post,
               n_mix_pre=n_mix_pre, w_mix_in=w_mix_in, w_gdn_conv=w_gdn_conv, gdn_a_log=gdn_a_log,
               gdn_dt_bias=gdn_dt_bias, gdn_norm=gdn_norm, w_gdn_out=w_gdn_out, w_sc_conv=w_sc_conv,
               w_sc_out=w_sc_out, w_mix_out=w_mix_out, n_mix_post=n_mix_post, n_x_pre=n_x_pre,
               n_mem=n_mem, w_xq=w_xq, w_xkv=w_xkv, w_xo=w_xo, n_x_post=n_x_post,
               n_ffn2_pre=n_ffn2_pre, w_ffn2_in=w_ffn2_in, w_ffn2_out=w_ffn2_out, n_ffn2_post=n_ffn2_post)
    assert w_mix_in.shape[0] == 1, "single-layer model"
    p = _prep_params(raw)
    bp = x_prompt.shape[0]
    mk_p, mv_p = _memkv_call(mem_prompt.reshape(bp * N_MEM, D_MODEL), p['n_mem'], p['w_xkv'])
    mk_p = mk_p.reshape(bp, N_MEM, XA_HEADS, XA_DIM)
    mv_p = mv_p.reshape(bp, N_MEM, XA_HEADS, XA_DIM)
    s0 = jnp.zeros((bp, GDN_HEADS, GDN_DK, GDN_DK), F32)
    gb0 = jnp.zeros((bp, GDN_CONV - 1, GDN_QKV), F32)
    sb0 = jnp.zeros((bp, SC_CONV - 1, D_MODEL), F32)
    yp, sp, gbp, sbp = _layer(x_prompt, mk_p, mv_p, s0, gb0, sb0, p)
    kv_shape = cache_mem_k.shape[1:]
    ys, ss, gbs, sbs = _layer(x_sample, cache_mem_k.reshape(kv_shape), cache_mem_v.reshape(kv_shape),
                              state_gdn[0], state_gdn_conv[0], state_shortconv[0], p)
    return (yp, ys, sp[None], gbp[None], sbp[None], mk_p[None], mv_p[None], ss[None], gbs[None], sbs[None])
```
